```python
import math
import jax
import jax.numpy as jnp
from jax import lax
import numpy as np

D_MODEL = 4096
BATCH = 32
SEQ = 256
DEPTH = 1
DEC_BATCH = 2
DEC_SEQ = 4096
PAST_LEN = 512

GRID_W = 64
NORM_EPS = 1e-6
ROPE_THETA = 10000.0

MLA_HEADS = 16
QK_NOPE = 128
QK_ROPE = 64
V_HEAD = 128
Q_LORA = 768
KV_LORA = 512
MLA_WIDTH = MLA_HEADS * V_HEAD
QK_SCALE = (QK_NOPE + QK_ROPE) ** -0.5
Q_BLOCK = 128

SSM_HEADDIM = 64
SSM_HEADS = 32
SSM_WIDTH = SSM_HEADS * SSM_HEADDIM
SSM_GROUPS = 4
D_STATE = 128
D_CONV = 5
CHUNK = 128
CONV_DIM = SSM_WIDTH + 2 * SSM_GROUPS * D_STATE

MIX_WIDTH = MLA_WIDTH + SSM_WIDTH
IN_SPLITS = (Q_LORA, Q_LORA + KV_LORA, Q_LORA + KV_LORA + QK_ROPE,
             Q_LORA + KV_LORA + QK_ROPE + SSM_WIDTH,
             Q_LORA + KV_LORA + QK_ROPE + SSM_WIDTH + CONV_DIM)
IN_PROJ = Q_LORA + KV_LORA + QK_ROPE + SSM_WIDTH + CONV_DIM + 2 * SSM_HEADS

N_EXPERTS = 32
TOP_K = 4
D_EXPERT = D_MODEL
SWIGLU_LIMIT = 7.0
SWIGLU_ALPHA = 1.702
EXPERT_BLOCK = 128

kernel_name = 'hybrid_mla_ssd_moe_diffusion_step'


def _rms(x, w):
    xf = x.astype(jnp.float32)
    xf = xf * lax.rsqrt(jnp.mean(xf * xf, axis=-1, keepdims=True) + NORM_EPS)
    return (xf * w.astype(jnp.float32)).astype(x.dtype)


def _rope_2d_tables(n_tok, dtype):
    rows = n_tok // GRID_W
    row = jnp.repeat(jnp.arange(rows), GRID_W).astype(jnp.float32)
    col = jnp.tile(jnp.arange(GRID_W), rows).astype(jnp.float32)
    half = QK_ROPE // 2
    inv = ROPE_THETA ** (-jnp.arange(0, half, 2, dtype=jnp.float32) / half)
    ang_r = row[:, None] * inv
    ang_c = col[:, None] * inv
    ang = jnp.concatenate([ang_r, ang_r, ang_c, ang_c], axis=-1)
    return jnp.cos(ang).astype(dtype), jnp.sin(ang).astype(dtype)


def _rope_2d(x, cos, sin):
    a1, a2, b1, b2 = jnp.split(x, 4, axis=-1)
    rot = jnp.concatenate([-a2, a1, -b2, b1], axis=-1)
    return x * cos + rot * sin


def _attend(q_nope, q_rope, k_nope, k_rope, v):
    b, s, h, _ = q_nope.shape
    nb = s // Q_BLOCK

    def blocks(t):
        return jnp.moveaxis(t.reshape(b, nb, Q_BLOCK, h, t.shape[-1]), 1, 0)

    def one_block(qs):
        qn, qr = qs
        logits = (jnp.einsum('bqhd,bkhd->bhqk', qn, k_nope)
                  + jnp.einsum('bqhr,bkr->bhqk', qr, k_rope)).astype(jnp.float32) * QK_SCALE
        probs = jax.nn.softmax(logits, axis=-1).astype(v.dtype)
        return jnp.einsum('bhqk,bkhd->bqhd', probs, v)

    out = lax.map(one_block, (blocks(q_nope), blocks(q_rope)))
    return jnp.moveaxis(out, 0, 1).reshape(b, s, h * V_HEAD)


def _dwconv_silu(u, w, bias):
    out = lax.conv_general_dilated(u, w[:, None, :].astype(u.dtype), window_strides=(1,),
                                   padding=[(D_CONV // 2, D_CONV // 2)],
                                   dimension_numbers=('NWC', 'WIO', 'NWC'),
                                   feature_group_count=u.shape[-1])
    return jax.nn.silu(out + bias)


def _segsum(a):
    t = a.shape[-1]
    a_rep = jnp.broadcast_to(a[..., None], a.shape + (t,))
    strict = jnp.tril(jnp.ones((t, t), dtype=bool), -1)
    seg = jnp.cumsum(jnp.where(strict, a_rep, 0.0), axis=-2)
    return jnp.where(jnp.tril(jnp.ones((t, t), dtype=bool)), seg, -jnp.inf)


def _ssd(xs, dt, a_coef, bm, cm, init):
    b, l, h, p = xs.shape
    c = l // CHUNK
    rep = h // SSM_GROUPS
    f32 = jnp.float32
    bh = jnp.repeat(bm, rep, axis=2).astype(f32).reshape(b, c, CHUNK, h, D_STATE)
    ch = jnp.repeat(cm, rep, axis=2).astype(f32).reshape(b, c, CHUNK, h, D_STATE)
    xdt = (xs.astype(f32) * dt[..., None]).reshape(b, c, CHUNK, h, p)
    a = (dt * a_coef.astype(f32)).reshape(b, c, CHUNK, h).transpose(0, 3, 1, 2)
    a_cs = jnp.cumsum(a, axis=-1)
    decay_in = jnp.exp(_segsum(a))
    cb = jnp.einsum('bcqhn,bcshn->bhcqs', ch, bh)
    y_diag = jnp.einsum('bhcqs,bcshp->bcqhp', cb * decay_in, xdt)
    decay_states = jnp.exp(a_cs[..., -1:] - a_cs)
    states = jnp.einsum('bcshn,bhcs,bcshp->bchpn', bh, decay_states, xdt)
    states = jnp.concatenate([init.astype(f32)[:, None], states], axis=1)
    chunk_a = jnp.pad(a_cs[..., -1], ((0, 0), (0, 0), (1, 0)))
    chunk_decay = jnp.exp(_segsum(chunk_a))
    new_states = jnp.einsum('bhzc,bchpn->bzhpn', chunk_decay, states)
    prev_states, final = new_states[:, :-1], new_states[:, -1]
    y_off = jnp.einsum('bcqhn,bchpn,bhcq->bcqhp', ch, prev_states, jnp.exp(a_cs))
    y = (y_diag + y_off).reshape(b, l, h, p)
    return y.astype(xs.dtype), final.astype(xs.dtype)


def _ssm(z, xbc, dt_raw, init_f, init_b, conv_w, conv_b, dt_bias_fwd, dt_bias_bwd,
         a_log_fwd, a_log_bwd, d_skip, norm_ssm):
    b, l, _ = xbc.shape
    xbc = _dwconv_silu(xbc, conv_w, conv_b)
    xs, bm, cm = jnp.split(xbc, [SSM_WIDTH, SSM_WIDTH + SSM_GROUPS * D_STATE], axis=-1)
    xs = xs.reshape(b, l, SSM_HEADS, SSM_HEADDIM)
    bm = bm.reshape(b, l, SSM_GROUPS, D_STATE)
    cm = cm.reshape(b, l, SSM_GROUPS, D_STATE)
    dt_f = jax.nn.softplus((dt_raw[..., :SSM_HEADS] + dt_bias_fwd).astype(jnp.float32))
    dt_b = jax.nn.softplus((dt_raw[..., SSM_HEADS:] + dt_bias_bwd).astype(jnp.float32))
    y_f, fin_f = _ssd(xs, dt_f, -jnp.exp(a_log_fwd), bm, cm, init_f)
    flip = lambda t: jnp.flip(t, axis=1)
    y_b, fin_b = _ssd(flip(xs), flip(dt_b), -jnp.exp(a_log_bwd), flip(bm), flip(cm), init_b)
    y = y_f + flip(y_b) + d_skip[:, None] * xs
    y = y.reshape(b, l, SSM_WIDTH) * jax.nn.silu(z)
    y = _rms(y.reshape(b, l, SSM_GROUPS, SSM_WIDTH // SSM_GROUPS),
             norm_ssm.reshape(SSM_GROUPS, SSM_WIDTH // SSM_GROUPS)).reshape(b, l, SSM_WIDTH)
    return y, fin_f, fin_b


def _moe(h, router_w, router_b, w_gate, b_gate, w_up, b_up, w_down, b_down):
    shp = h.shape
    t = h.reshape(-1, shp[-1])
    n = t.shape[0]
    logits = (t @ router_w + router_b).astype(jnp.float32)
    top_v, top_i = lax.top_k(logits, TOP_K)
    gates = jax.nn.softmax(top_v, axis=-1).astype(h.dtype)
    m = n * TOP_K
    flat_e = top_i.reshape(-1)
    flat_tok = jnp.repeat(jnp.arange(n, dtype=jnp.int32), TOP_K)
    order = jnp.argsort(flat_e)
    sorted_e = flat_e[order]
    counts = jnp.bincount(flat_e, length=N_EXPERTS)
    start = jnp.cumsum(counts) - counts
    pcounts = (counts + EXPERT_BLOCK - 1) // EXPERT_BLOCK * EXPERT_BLOCK
    pend = jnp.cumsum(pcounts)
    pstart = pend - pcounts
    dest = pstart[sorted_e] + jnp.arange(m, dtype=jnp.int32) - start[sorted_e]
    n_blocks = -(-m // EXPERT_BLOCK) + N_EXPERTS
    rows = n_blocks * EXPERT_BLOCK
    row_tok = jnp.full((rows,), n, dtype=jnp.int32).at[dest].set(flat_tok[order])
    row_gate = jnp.zeros((rows,), h.dtype).at[dest].set(gates.reshape(-1)[order])
    block_e = jnp.minimum(jnp.searchsorted(pend, jnp.arange(n_blocks, dtype=jnp.int32) * EXPERT_BLOCK,
                                           side='right'), N_EXPERTS - 1)
    t_pad = jnp.concatenate([t, jnp.zeros((1, shp[-1]), t.dtype)], axis=0)
    xb = t_pad[row_tok].reshape(n_blocks, EXPERT_BLOCK, shp[-1])

    def expert_block(args):
        xblk, e = args
        g = jnp.minimum(xblk @ w_gate[e] + b_gate[e], SWIGLU_LIMIT)
        u = jnp.clip(xblk @ w_up[e] + b_up[e], -SWIGLU_LIMIT, SWIGLU_LIMIT)
        act = g * jax.nn.sigmoid(SWIGLU_ALPHA * g) * (u + 1.0)
        return act @ w_down[e] + b_down[e]

    yb = lax.map(expert_block, (xb, block_e)).reshape(rows, shp[-1])
    y = jax.ops.segment_sum(yb * row_gate[:, None], row_tok, num_segments=n + 1)[:n]
    return y.reshape(shp)


def _layer(x, mod, cos, sin, ctx_ckv, ctx_kr, init_f, init_b,
           norm_mix_pre, norm_mix_post, norm_ffn_pre, norm_ffn_post,
           w_in, norm_q, norm_kv, w_uq, w_uk, w_uv, conv_w, conv_b,
           dt_bias_fwd, dt_bias_bwd, a_log_fwd, a_log_bwd, d_skip, norm_ssm, w_out,
           router_w, router_b, w_gate, b_gate, w_up, b_up, w_down, b_down):
    b, s, _ = x.shape
    shift1, scale1, gate1, shift2, scale2, gate2 = jnp.split(mod, 6, axis=-1)
    h = _rms(x, norm_mix_pre) * (1.0 + scale1) + shift1
    cq, ckv, kr, z, xbc, dt_raw = jnp.split(h @ w_in, IN_SPLITS, axis=-1)
    ckv = _rms(ckv, norm_kv)
    q = (_rms(cq, norm_q) @ w_uq).reshape(b, s, MLA_HEADS, QK_NOPE + QK_ROPE)
    q_nope, q_rope = q[..., :QK_NOPE], q[..., QK_NOPE:]
    k_nope = (ckv @ w_uk).reshape(b, s, MLA_HEADS, QK_NOPE)
    v = (ckv @ w_uv).reshape(b, s, MLA_HEADS, V_HEAD)
    k_rope = kr
    if cos is not None:
        n_ctx = ctx_ckv.shape[1]
        q_rope = _rope_2d(q_rope, cos[:, None], sin[:, None])
        k_nope = jnp.concatenate([k_nope, (ctx_ckv @ w_uk).reshape(b, n_ctx, MLA_HEADS, QK_NOPE)], axis=1)
        v = jnp.concatenate([v, (ctx_ckv @ w_uv).reshape(b, n_ctx, MLA_HEADS, V_HEAD)], axis=1)
        k_rope = jnp.concatenate([_rope_2d(kr, cos, sin), ctx_kr], axis=1)
    attn = _attend(q_nope, q_rope, k_nope, k_rope, v)
    ssm, fin_f, fin_b = _ssm(z, xbc, dt_raw, init_f, init_b, conv_w, conv_b, dt_bias_fwd, dt_bias_bwd,
                             a_log_fwd, a_log_bwd, d_skip, norm_ssm)
    mix = jnp.concatenate([attn, ssm], axis=-1) @ w_out
    x = x + gate1 * _rms(mix, norm_mix_post)
    h = _rms(x, norm_ffn_pre) * (1.0 + scale2) + shift2
    x = x + gate2 * _rms(_moe(h, router_w, router_b, w_gate, b_gate, w_up, b_up, w_down, b_down), norm_ffn_post)
    return x, ckv, kr, fin_f, fin_b


def setup_inputs(seed: int = 0) -> dict:
    key = jax.random.key(seed)
    ks = iter(jax.random.split(key, 48))
    f32 = jnp.float32

    def nrm(shape, scale=1.0):
        return jax.random.normal(next(ks), shape, f32) * scale

    def gain(n):
        return 1.0 + nrm((DEPTH, n), 0.05)

    L = DEPTH
    dt0 = jnp.exp(jax.random.uniform(next(ks), (L, 2, SSM_HEADS), f32, math.log(1e-3), math.log(1e-1)))
    dt_bias = dt0 + jnp.log(-jnp.expm1(-dt0))
    a_log = jnp.log(jax.random.uniform(next(ks), (L, 2, SSM_HEADS), f32, 1.0, 16.0))
    return {
        'x_prompt': nrm((BATCH, SEQ, D_MODEL)),
        'x_sample': nrm((DEC_BATCH, DEC_SEQ, D_MODEL)),
        'cache_ckv': nrm((DEC_BATCH, L, PAST_LEN, KV_LORA)),
        'cache_krope': nrm((DEC_BATCH, L, PAST_LEN, QK_ROPE)),
        'state_ssm_fwd': nrm((DEC_BATCH, L, SSM_HEADS, SSM_HEADDIM, D_STATE), 0.1),
        'state_ssm_bwd': nrm((DEC_BATCH, L, SSM_HEADS, SSM_HEADDIM, D_STATE), 0.1),
        'c': nrm((DEC_BATCH, D_MODEL)),
        'c_ctx': nrm((D_MODEL,)),
        'w_mod': nrm((L, D_MODEL, 6 * D_MODEL), 0.5 * D_MODEL ** -0.5),
        'b_mod': nrm((L, 6 * D_MODEL), 0.01),
        'norm_mix_pre': gain(D_MODEL),
        'norm_mix_post': gain(D_MODEL),
        'norm_ffn_pre': gain(D_MODEL),
        'norm_ffn_post': gain(D_MODEL),
        'w_in': nrm((L, D_MODEL, IN_PROJ), D_MODEL ** -0.5),
        'norm_q': gain(Q_LORA),
        'norm_kv': gain(KV_LORA),
        'w_uq': nrm((L, Q_LORA, MLA_HEADS * (QK_NOPE + QK_ROPE)), Q_LORA ** -0.5),
        'w_uk': nrm((L, KV_LORA, MLA_HEADS * QK_NOPE), KV_LORA ** -0.5),
        'w_uv': nrm((L, KV_LORA, MLA_HEADS * V_HEAD), KV_LORA ** -0.5),
        'conv_w': nrm((L, D_CONV, CONV_DIM), D_CONV ** -0.5),
        'conv_b': nrm((L, CONV_DIM), 0.01),
        'dt_bias_fwd': dt_bias[:, 0],
        'dt_bias_bwd': dt_bias[:, 1],
        'a_log_fwd': a_log[:, 0],
        'a_log_bwd': a_log[:, 1],
        'd_skip': gain(SSM_HEADS),
        'norm_ssm': gain(SSM_WIDTH),
        'w_out': nrm((L, MIX_WIDTH, D_MODEL), MIX_WIDTH ** -0.5),
        'router_w': nrm((L, D_MODEL, N_EXPERTS), D_MODEL ** -0.5),
        'router_b': nrm((L, N_EXPERTS), 0.01),
        'w_gate': nrm((L, N_EXPERTS, D_MODEL, D_EXPERT), D_MODEL ** -0.5),
        'b_gate': nrm((L, N_EXPERTS, D_EXPERT), 0.01),
        'w_up': nrm((L, N_EXPERTS, D_MODEL, D_EXPERT), D_MODEL ** -0.5),
        'b_up': nrm((L, N_EXPERTS, D_EXPERT), 0.01),
        'w_down': nrm((L, N_EXPERTS, D_EXPERT, D_MODEL), D_EXPERT ** -0.5),
        'b_down': nrm((L, N_EXPERTS, D_MODEL), 0.01),
    }


def reference(x_prompt, x_sample, cache_ckv, cache_krope, state_ssm_fwd, state_ssm_bwd, c, c_ctx,
              w_mod, b_mod, norm_mix_pre, norm_mix_post, norm_ffn_pre, norm_ffn_post,
              w_in, norm_q, norm_kv, w_uq, w_uk, w_uv, conv_w, conv_b,
              dt_bias_fwd, dt_bias_bwd, a_log_fwd, a_log_bwd, d_skip, norm_ssm, w_out,
              router_w, router_b, w_gate, b_gate, w_up, b_up, w_down, b_down):
    rope_cos, rope_sin = _rope_2d_tables(x_sample.shape[1], x_sample.dtype)
    zero_state = jnp.zeros((x_prompt.shape[0], SSM_HEADS, SSM_HEADDIM, D_STATE), x_prompt.dtype)
    xp, xl = x_prompt, x_sample
    ckv_list, kr_list, sf_list, sb_list = [], [], [], []
    for l in range(DEPTH):
        p = (norm_mix_pre[l], norm_mix_post[l], norm_ffn_pre[l], norm_ffn_post[l],
             w_in[l], norm_q[l], norm_kv[l], w_uq[l], w_uk[l], w_uv[l], conv_w[l], conv_b[l],
             dt_bias_fwd[l], dt_bias_bwd[l], a_log_fwd[l], a_log_bwd[l], d_skip[l], norm_ssm[l], w_out[l],
             router_w[l], router_b[l], w_gate[l], b_gate[l], w_up[l], b_up[l], w_down[l], b_down[l])
        mod_ctx = (jax.nn.silu(c_ctx) @ w_mod[l] + b_mod[l])[None, None, :]
        mod_lat = (jax.nn.silu(c) @ w_mod[l] + b_mod[l])[:, None, :]
        xp, ckv, kr, sf, sb = _layer(xp, mod_ctx, None, None, None, None, zero_state, zero_state, *p)
        ckv_list.append(ckv)
        kr_list.append(kr)
        sf_list.append(sf)
        sb_list.append(sb)
        xl = _layer(xl, mod_lat, rope_cos, rope_sin, cache_ckv[:, l], cache_krope[:, l],
                    state_ssm_fwd[:, l], state_ssm_bwd[:, l], *p)[0]
    y_prompt = xp
    y_sample = xl
    new_cache_ckv = jnp.stack(ckv_list, axis=1)
    new_cache_krope = jnp.stack(kr_list, axis=1)
    new_state_ssm_fwd = jnp.stack(sf_list, axis=1)
    new_state_ssm_bwd = jnp.stack(sb_list, axis=1)
    return (y_prompt, y_sample, new_cache_ckv, new_cache_krope, new_state_ssm_fwd, new_state_ssm_bwd)
```

```python
import functools
import math

import jax
import jax.numpy as jnp
from jax import lax
from jax.experimental import pallas as pl
from jax.experimental.pallas import tpu as pltpu

F32 = jnp.float32
BF16 = jnp.bfloat16
HIGHEST = lax.Precision.HIGHEST

D_MODEL = 4096
GRID_W = 64
NORM_EPS = 1e-6
ROPE_THETA = 10000.0
MLA_HEADS = 16
QK_NOPE = 128
QK_ROPE = 64
V_HEAD = 128
Q_LORA = 768
KV_LORA = 512
MLA_WIDTH = MLA_HEADS * V_HEAD
QK_SCALE = (QK_NOPE + QK_ROPE) ** -0.5
SSM_HEADDIM = 64
SSM_HEADS = 32
SSM_WIDTH = SSM_HEADS * SSM_HEADDIM
SSM_GROUPS = 4
HEADS_PER_GROUP = SSM_HEADS // SSM_GROUPS
GROUP_WIDTH = SSM_WIDTH // SSM_GROUPS
D_STATE = 128
D_CONV = 5
CHUNK = 128
CONV_DIM = SSM_WIDTH + 2 * SSM_GROUPS * D_STATE
N_EXPERTS = 32
TOP_K = 4
SWIGLU_LIMIT = 7.0
SWIGLU_ALPHA = 1.702

LANES = 128
SUBLANES = 8
V7X_VMEM_LIMIT_BYTES = 56 * 1024 * 1024

QK_PAD = 2 * LANES
EXPERT_ROWS = 1024
ROW_CHUNK = 256
COMBINE_TOKENS = 64


def _cparams(sem, vmem=V7X_VMEM_LIMIT_BYTES):
    return pltpu.CompilerParams(dimension_semantics=sem, vmem_limit_bytes=vmem)


def _sigmoid(x):
    return 1.0 / (1.0 + jnp.exp(-x))


def _rms_rows(x):
    return x * lax.rsqrt(jnp.mean(x * x, axis=-1, keepdims=True) + NORM_EPS)


def _mod_kernel(c_ref, w_ref, b_ref, o_ref):
    c = c_ref[...]
    s = (c * _sigmoid(c)).astype(BF16)
    o_ref[...] = jnp.dot(s, w_ref[...].astype(BF16), preferred_element_type=F32) + b_ref[...]


def _modulation(cond, w_mod, b_mod, tn=512):
    rows, k = cond.shape
    n = w_mod.shape[1]
    return pl.pallas_call(
        _mod_kernel,
        grid=(n // tn,),
        in_specs=[pl.BlockSpec((rows, k), lambda j: (0, 0)),
                  pl.BlockSpec((k, tn), lambda j: (0, j)),
                  pl.BlockSpec((1, tn), lambda j: (0, j))],
        out_specs=pl.BlockSpec((rows, tn), lambda j: (0, j)),
        out_shape=jax.ShapeDtypeStruct((rows, n), F32),
        compiler_params=_cparams(("parallel",)),
        name="modulation",
    )(cond, w_mod, b_mod.reshape(1, n))


def _prenorm_kernel(x_ref, g_ref, sc_ref, sh_ref, o_ref):
    h = _rms_rows(x_ref[...]) * g_ref[...]
    o_ref[...] = (h * (1.0 + sc_ref[0]) + sh_ref[0]).astype(o_ref.dtype)


def _prenorm(x, gain, scale, shift, group_of_block, tm=256):
    m, d = x.shape
    return pl.pallas_call(
        _prenorm_kernel,
        grid=(m // tm,),
        in_specs=[pl.BlockSpec((tm, d), lambda i: (i, 0)),
                  pl.BlockSpec((1, d), lambda i: (0, 0)),
                  pl.BlockSpec((1, 1, d), lambda i: (group_of_block(i, tm), 0, 0)),
                  pl.BlockSpec((1, 1, d), lambda i: (group_of_block(i, tm), 0, 0))],
        out_specs=pl.BlockSpec((tm, d), lambda i: (i, 0)),
        out_shape=jax.ShapeDtypeStruct((m, d), BF16),
        compiler_params=_cparams(("parallel",)),
        name="prenorm",
    )(x, gain.reshape(1, d), scale, shift)


def _mm_kernel(a_ref, b_ref, o_ref):
    for r in range(a_ref.shape[0] // ROW_CHUNK):
        rows = slice(r * ROW_CHUNK, (r + 1) * ROW_CHUNK)
        o_ref[rows, :] = jnp.dot(a_ref[rows, :], b_ref[...], preferred_element_type=F32).astype(o_ref.dtype)


def _matmul(a, b, out_dtype, tm, tn, name):
    m, k = a.shape
    n = b.shape[1]
    return pl.pallas_call(
        _mm_kernel,
        grid=(m // tm, n // tn),
        in_specs=[pl.BlockSpec((tm, k), lambda i, j: (i, 0)),
                  pl.BlockSpec((k, tn), lambda i, j: (0, j))],
        out_specs=pl.BlockSpec((tm, tn), lambda i, j: (i, j)),
        out_shape=jax.ShapeDtypeStruct((m, n), out_dtype),
        compiler_params=_cparams(("parallel", "parallel")),
        name=name,
    )(a, b)


def _mm2_kernel(a1_ref, a2_ref, b1_ref, b2_ref, o_ref):
    for r in range(a1_ref.shape[0] // ROW_CHUNK):
        rows = slice(r * ROW_CHUNK, (r + 1) * ROW_CHUNK)
        acc = jnp.dot(a1_ref[rows, :], b1_ref[...], preferred_element_type=F32)
        acc = acc + jnp.dot(a2_ref[rows, :], b2_ref[...], preferred_element_type=F32)
        o_ref[rows, :] = acc.astype(o_ref.dtype)


def _matmul2(a1, a2, b1, b2, out_dtype, tm, tn, name):
    m, k1 = a1.shape
    k2 = a2.shape[1]
    n = b1.shape[1]
    return pl.pallas_call(
        _mm2_kernel,
        grid=(m // tm, n // tn),
        in_specs=[pl.BlockSpec((tm, k1), lambda i, j: (i, 0)),
                  pl.BlockSpec((tm, k2), lambda i, j: (i, 0)),
                  pl.BlockSpec((k1, tn), lambda i, j: (0, j)),
                  pl.BlockSpec((k2, tn), lambda i, j: (0, j))],
        out_specs=pl.BlockSpec((tm, tn), lambda i, j: (i, j)),
        out_shape=jax.ShapeDtypeStruct((m, n), out_dtype),
        compiler_params=_cparams(("parallel", "parallel")),
        name=name,
    )(a1, a2, b1, b2)


def _q_kernel(cq_ref, g_ref, w_ref, tab_ref, o_ref):
    xn = (_rms_rows(cq_ref[...]) * g_ref[...]).astype(BF16)
    tab = tab_ref[...]
    for h in range(MLA_HEADS):
        sl = slice(h * QK_PAD, (h + 1) * QK_PAD)
        q = jnp.dot(xn, w_ref[:, sl], preferred_element_type=F32)
        o_ref[:, sl] = (q * tab).astype(o_ref.dtype)


def _q_proj(cq, norm_q, w_q, q_table, tm=512):
    m, k = cq.shape
    n = w_q.shape[1]
    return pl.pallas_call(
        _q_kernel,
        grid=(m // tm,),
        in_specs=[pl.BlockSpec((tm, k), lambda i: (i, 0)),
                  pl.BlockSpec((1, k), lambda i: (0, 0)),
                  pl.BlockSpec((k, n), lambda i: (0, 0)),
                  pl.BlockSpec((tm, QK_PAD), lambda i: (i, 0))],
        out_specs=pl.BlockSpec((tm, n), lambda i: (i, 0)),
        out_shape=jax.ShapeDtypeStruct((m, n), BF16),
        compiler_params=_cparams(("parallel",)),
        name="q_proj",
    )(cq, norm_q.reshape(1, k), w_q, q_table)


def _kv_kernel(ckv_ref, g_ref, wk_ref, wv_ref, n_ref, k_ref, v_ref, *, normalise):
    x = ckv_ref[...]
    if normalise:
        x = _rms_rows(x) * g_ref[...]
    n_ref[...] = x
    xb = x.astype(BF16)
    k_ref[...] = jnp.dot(xb, wk_ref[...], preferred_element_type=F32).astype(k_ref.dtype)
    v_ref[...] = jnp.dot(xb, wv_ref[...], preferred_element_type=F32).astype(v_ref.dtype)


def _kv_proj(ckv, norm_kv, w_k, w_v, normalise, name, tm=512):
    m, k = ckv.shape
    n = w_k.shape[1]
    return pl.pallas_call(
        functools.partial(_kv_kernel, normalise=normalise),
        grid=(m // tm,),
        in_specs=[pl.BlockSpec((tm, k), lambda i: (i, 0)),
                  pl.BlockSpec((1, k), lambda i: (0, 0)),
                  pl.BlockSpec((k, n), lambda i: (0, 0)),
                  pl.BlockSpec((k, n), lambda i: (0, 0))],
        out_specs=[pl.BlockSpec((tm, k), lambda i: (i, 0)),
                   pl.BlockSpec((tm, n), lambda i: (i, 0)),
                   pl.BlockSpec((tm, n), lambda i: (i, 0))],
        out_shape=[jax.ShapeDtypeStruct((m, k), F32),
                   jax.ShapeDtypeStruct((m, n), BF16),
                   jax.ShapeDtypeStruct((m, n), BF16)],
        compiler_params=_cparams(("parallel",)),
        name=name,
    )(ckv, norm_kv.reshape(1, k), w_k, w_v)


def _krope_kernel(kr_ref, tab_ref, o_ref):
    t = kr_ref[...] * tab_ref[...]
    o_ref[...] = (t + pltpu.roll(t, QK_ROPE, 1)).astype(o_ref.dtype)


def _krope(krdt, k_table, tm=1024):
    m = krdt.shape[0]
    return pl.pallas_call(
        _krope_kernel,
        grid=(m // tm,),
        in_specs=[pl.BlockSpec((tm, LANES), lambda i: (i, 0)),
                  pl.BlockSpec((tm, LANES), lambda i: (i, 0))],
        out_specs=pl.BlockSpec((tm, LANES), lambda i: (i, 0)),
        out_shape=jax.ShapeDtypeStruct((m, LANES), BF16),
        compiler_params=_cparams(("parallel",)),
        name="krope",
    )(krdt, k_table)


def _attn_kernel(q_ref, kn_ref, kk_ref, v_ref, o_ref, m_ref, l_ref, acc_ref):
    ki = pl.program_id(3)

    @pl.when(ki == 0)
    def _():
        m_ref[...] = jnp.full(m_ref.shape, -jnp.inf, F32)
        l_ref[...] = jnp.zeros(l_ref.shape, F32)
        acc_ref[...] = jnp.zeros(acc_ref.shape, F32)

    k = jnp.concatenate([kn_ref[...], kk_ref[...]], axis=1)
    s = lax.dot_general(q_ref[...], k, (((1,), (1,)), ((), ())), preferred_element_type=F32)
    m_old = m_ref[...]
    m_new = jnp.maximum(m_old, jnp.max(s, axis=1, keepdims=True))
    alpha = jnp.exp(m_old - m_new)
    p = jnp.exp(s - m_new)
    l_ref[...] = alpha * l_ref[...] + jnp.sum(p, axis=1, keepdims=True)
    acc_ref[...] = alpha * acc_ref[...] + jnp.dot(p.astype(BF16), v_ref[...], preferred_element_type=F32)
    m_ref[...] = m_new

    @pl.when(ki == pl.num_programs(3) - 1)
    def _():
        o_ref[...] = (acc_ref[...] / l_ref[...]).astype(o_ref.dtype)


def _attention(q, kn, kk, v, n_batch, s_len, kv_len, q_row0, tq, tk, name):
    nq = s_len // tq
    nk = kv_len // tk
    q_blk0 = q_row0 // tq
    return pl.pallas_call(
        _attn_kernel,
        grid=(n_batch, MLA_HEADS, nq, nk),
        in_specs=[pl.BlockSpec((tq, QK_PAD), lambda b, h, qi, ki: (q_blk0 + b * nq + qi, h)),
                  pl.BlockSpec((tk, QK_NOPE), lambda b, h, qi, ki: (b * nk + ki, h)),
                  pl.BlockSpec((tk, LANES), lambda b, h, qi, ki: (b * nk + ki, 0)),
                  pl.BlockSpec((tk, V_HEAD), lambda b, h, qi, ki: (b * nk + ki, h))],
        out_specs=pl.BlockSpec((tq, V_HEAD), lambda b, h, qi, ki: (b * nq + qi, h)),
        out_shape=jax.ShapeDtypeStruct((n_batch * s_len, MLA_WIDTH), BF16),
        scratch_shapes=[pltpu.VMEM((tq, 1), F32), pltpu.VMEM((tq, 1), F32),
                        pltpu.VMEM((tq, V_HEAD), F32)],
        compiler_params=_cparams(("parallel", "parallel", "parallel", "arbitrary")),
        name=name,
    )(q, kn, kk, v)


def _conv_kernel(u_ref, w_ref, b_ref, o_ref, pad_ref, *, seq_len):
    halo = SUBLANES
    width = u_ref.shape[-1]
    pad_ref[0:halo, :] = jnp.zeros((halo, width), F32)
    pad_ref[halo + seq_len:2 * halo + seq_len, :] = jnp.zeros((halo, width), F32)
    pad_ref[halo:halo + seq_len, :] = u_ref[...]
    for r in range(seq_len // ROW_CHUNK):
        acc = b_ref[...]
        for k in range(D_CONV):
            start = halo - D_CONV // 2 + k + r * ROW_CHUNK
            acc = acc + w_ref[k:k + 1, :] * pad_ref[start:start + ROW_CHUNK, :]
        o_ref[r * ROW_CHUNK:(r + 1) * ROW_CHUNK, :] = acc * _sigmoid(acc)


def _conv_silu(xbc, conv_w, conv_b, n_seq, seq_len, row0, tc, name):
    width = xbc.shape[1]
    blk0 = row0 // seq_len
    return pl.pallas_call(
        functools.partial(_conv_kernel, seq_len=seq_len),
        grid=(n_seq, width // tc),
        in_specs=[pl.BlockSpec((seq_len, tc), lambda s, c: (blk0 + s, c)),
                  pl.BlockSpec((D_CONV, tc), lambda s, c: (0, c)),
                  pl.BlockSpec((1, tc), lambda s, c: (0, c))],
        out_specs=pl.BlockSpec((seq_len, tc), lambda s, c: (s, c)),
        out_shape=jax.ShapeDtypeStruct((n_seq * seq_len, width), F32),
        scratch_shapes=[pltpu.VMEM((seq_len + 2 * SUBLANES, tc), F32)],
        compiler_params=_cparams(("parallel", "parallel")),
        name=name,
    )(xbc, conv_w, conv_b.reshape(1, width))


def _dt_kernel(raw_ref, bias_ref, alog_ref, dt_ref, cs_ref, cst_ref):
    raw = raw_ref[...] + bias_ref[...]
    dt = jnp.maximum(raw, 0.0) + jnp.log1p(jnp.exp(-jnp.abs(raw)))
    a = dt * (-jnp.exp(alog_ref[...]))
    row = lax.broadcasted_iota(jnp.int32, (CHUNK, CHUNK), 0)
    col = lax.broadcasted_iota(jnp.int32, (CHUNK, CHUNK), 1)
    lower = (row >= col).astype(F32)
    upper = (row <= col).astype(F32)
    prefix = jnp.dot(lower, a, precision=HIGHEST, preferred_element_type=F32)
    suffix = jnp.dot(upper, a, precision=HIGHEST, preferred_element_type=F32)
    cs = jnp.where(col < SSM_HEADS, prefix, suffix)
    dt_ref[...] = dt
    cs_ref[...] = cs
    cst_ref[...] = cs.T


def _dt_prep(krdt, dt_bias, a_log):
    m = krdt.shape[0]
    shp = jax.ShapeDtypeStruct((m, LANES), F32)
    spec = pl.BlockSpec((CHUNK, LANES), lambda i: (i, 0))
    return pl.pallas_call(
        _dt_kernel,
        grid=(m // CHUNK,),
        in_specs=[pl.BlockSpec((CHUNK, LANES), lambda i: (i, 1)),
                  pl.BlockSpec((1, LANES), lambda i: (0, 0)),
                  pl.BlockSpec((1, LANES), lambda i: (0, 0))],
        out_specs=[spec, spec, spec],
        out_shape=[shp, shp, shp],
        compiler_params=_cparams(("parallel",)),
        name="dt_prep",
    )(krdt, dt_bias, a_log)


def _ssd_kernel(x_ref, b_ref, c_ref, dt_ref, cs_ref, cst_ref, init_ref, y_ref, fin_ref, st_ref,
                *, reverse, n_chunks):
    c = pl.program_id(1)

    @pl.when(c == 0)
    def _():
        st_ref[...] = init_ref[0].T

    col0 = SSM_HEADS if reverse else 0
    ii = lax.broadcasted_iota(jnp.int32, (CHUNK, CHUNK), 0)
    jj = lax.broadcasted_iota(jnp.int32, (CHUNK, CHUNK), 1)
    mask = (jj >= ii) if reverse else (ii >= jj)
    dt = dt_ref[...]
    cs = cs_ref[...]
    cst = cst_ref[...]
    total = cs[0:1, :] if reverse else cs[CHUNK - 1:CHUNK, :]
    for g in range(SSM_GROUPS):
        gs = slice(g * D_STATE, (g + 1) * D_STATE)
        b_f32 = b_ref[:, gs]
        bg = b_f32.astype(BF16)
        bgt = b_f32.T.astype(BF16)
        cg = c_ref[:, gs].astype(BF16)
        cb = lax.dot_general(cg, bg, (((1,), (1,)), ((), ())), preferred_element_type=F32)
        st_old = st_ref[:, g * GROUP_WIDTH:(g + 1) * GROUP_WIDTH]
        y_off = jnp.dot(cg, st_old.astype(BF16), preferred_element_type=F32)
        for j in range(HEADS_PER_GROUP):
            h = g * HEADS_PER_GROUP + j
            col = col0 + h
            hs = slice(h * SSM_HEADDIM, (h + 1) * SSM_HEADDIM)
            p_col = cs[:, col:col + 1]
            p_row = cst[col:col + 1, :]
            tot = total[:, col:col + 1]
            w = jnp.where(mask, jnp.exp(p_col - p_row), 0.0)
            xdt = x_ref[:, hs] * dt[:, col:col + 1]
            y_diag = jnp.dot((cb * w).astype(BF16), xdt.astype(BF16), preferred_element_type=F32)
            y_ref[:, hs] = y_diag + y_off[:, j * SSM_HEADDIM:(j + 1) * SSM_HEADDIM] * jnp.exp(p_col)
            xdec = (xdt * jnp.exp(tot - p_col)).astype(BF16)
            st_ref[:, hs] = (st_old[:, j * SSM_HEADDIM:(j + 1) * SSM_HEADDIM] * jnp.exp(tot)
                             + jnp.dot(bgt, xdec, preferred_element_type=F32))

    @pl.when(c == n_chunks - 1)
    def _():
        fin_ref[0] = st_ref[...].T


def _ssd_scan(xbc, dt, cs, cst, init, n_seq, seq_len, row0, reverse, name):
    n_chunks = seq_len // CHUNK
    blk0 = row0 // CHUNK
    xblk = SSM_WIDTH // (SSM_GROUPS * D_STATE)

    def chunk(s, c):
        return s * n_chunks + ((n_chunks - 1 - c) if reverse else c)

    return pl.pallas_call(
        functools.partial(_ssd_kernel, reverse=reverse, n_chunks=n_chunks),
        grid=(n_seq, n_chunks),
        in_specs=[pl.BlockSpec((CHUNK, SSM_WIDTH), lambda s, c: (chunk(s, c), 0)),
                  pl.BlockSpec((CHUNK, SSM_GROUPS * D_STATE), lambda s, c: (chunk(s, c), xblk)),
                  pl.BlockSpec((CHUNK, SSM_GROUPS * D_STATE), lambda s, c: (chunk(s, c), xblk + 1)),
                  pl.BlockSpec((CHUNK, LANES), lambda s, c: (blk0 + chunk(s, c), 0)),
                  pl.BlockSpec((CHUNK, LANES), lambda s, c: (blk0 + chunk(s, c), 0)),
                  pl.BlockSpec((CHUNK, LANES), lambda s, c: (blk0 + chunk(s, c), 0)),
                  pl.BlockSpec((1, SSM_WIDTH, D_STATE), lambda s, c: (s, 0, 0))],
        out_specs=[pl.BlockSpec((CHUNK, SSM_WIDTH), lambda s, c: (chunk(s, c), 0)),
                   pl.BlockSpec((1, SSM_WIDTH, D_STATE), lambda s, c: (s, 0, 0))],
        out_shape=[jax.ShapeDtypeStruct((n_seq * seq_len, SSM_WIDTH), F32),
                   jax.ShapeDtypeStruct((n_seq, SSM_WIDTH, D_STATE), F32)],
        scratch_shapes=[pltpu.VMEM((D_STATE, SSM_WIDTH), F32)],
        compiler_params=_cparams(("parallel", "arbitrary")),
        name=name,
    )(xbc, xbc, xbc, dt, cs, cst, init)


def _ssd_out_kernel(yf_ref, yb_ref, x_ref, z_ref, d_ref, g_ref, o_ref):
    for g in range(SSM_GROUPS):
        gs = slice(g * GROUP_WIDTH, (g + 1) * GROUP_WIDTH)
        z = z_ref[:, gs]
        y = (yf_ref[:, gs] + yb_ref[:, gs] + d_ref[:, gs] * x_ref[:, gs]) * (z * _sigmoid(z))
        o_ref[:, gs] = (_rms_rows(y) * g_ref[:, gs]).astype(o_ref.dtype)


def _ssd_out(y_f, y_b, xbc, z, d_wide, norm_ssm, row0, tm=256):
    m = y_f.shape[0]
    blk0 = row0 // tm
    return pl.pallas_call(
        _ssd_out_kernel,
        grid=(m // tm,),
        in_specs=[pl.BlockSpec((tm, SSM_WIDTH), lambda i: (i, 0)),
                  pl.BlockSpec((tm, SSM_WIDTH), lambda i: (i, 0)),
                  pl.BlockSpec((tm, SSM_WIDTH), lambda i: (i, 0)),
                  pl.BlockSpec((tm, SSM_WIDTH), lambda i: (blk0 + i, 0)),
                  pl.BlockSpec((1, SSM_WIDTH), lambda i: (0, 0)),
                  pl.BlockSpec((1, SSM_WIDTH), lambda i: (0, 0))],
        out_specs=pl.BlockSpec((tm, SSM_WIDTH), lambda i: (i, 0)),
        out_shape=jax.ShapeDtypeStruct((m, SSM_WIDTH), BF16),
        compiler_params=_cparams(("parallel",)),
        name="ssd_out",
    )(y_f, y_b, xbc, z, d_wide, norm_ssm.reshape(1, SSM_WIDTH))


def _postmix_kernel(x_ref, mix_ref, gpost_ref, gate1_ref, gpre_ref, sc_ref, sh_ref, rw_ref, rb_ref,
                    x1_ref, h_ref, idx_ref, gates_ref):
    x1 = x_ref[...] + gate1_ref[0] * (_rms_rows(mix_ref[...]) * gpost_ref[...])
    x1_ref[...] = x1
    h = (_rms_rows(x1) * gpre_ref[...]) * (1.0 + sc_ref[0]) + sh_ref[0]
    h_ref[...] = h
    logits = jnp.dot(h, rw_ref[...], precision=HIGHEST, preferred_element_type=F32) + rb_ref[...]
    lane = lax.broadcasted_iota(jnp.int32, logits.shape, 1)
    lane_f = lane.astype(F32)
    live = jnp.where(lane < N_EXPERTS, logits, -jnp.inf)
    vals, idxs = [], []
    for _ in range(TOP_K):
        top = jnp.max(live, axis=1, keepdims=True)
        first = jnp.min(jnp.where(live == top, lane_f, float(LANES)), axis=1, keepdims=True)
        vals.append(top)
        idxs.append(first)
        live = jnp.where(lane_f == first, -jnp.inf, live)
    exps = [jnp.exp(v - vals[0]) for v in vals]
    denom = exps[0]
    for e in exps[1:]:
        denom = denom + e
    gate_out = jnp.zeros(logits.shape, F32)
    idx_out = jnp.zeros(logits.shape, F32)
    for k in range(TOP_K):
        gate_out = jnp.where(lane == k, exps[k] / denom, gate_out)
        idx_out = jnp.where(lane == k, idxs[k], idx_out)
    gates_ref[...] = gate_out
    idx_ref[...] = idx_out.astype(jnp.int32)


def _postmix_router(x, mix, g_post, gate1, g_pre, scale2, shift2, router_w, router_b,
                    group_of_block, tm=128):
    m, d = x.shape
    row = lambda i: (i, 0)
    fixed = lambda i: (0, 0)
    grp = lambda i: (group_of_block(i, tm), 0, 0)
    return pl.pallas_call(
        _postmix_kernel,
        grid=(m // tm,),
        in_specs=[pl.BlockSpec((tm, d), row), pl.BlockSpec((tm, d), row),
                  pl.BlockSpec((1, d), fixed), pl.BlockSpec((1, 1, d), grp),
                  pl.BlockSpec((1, d), fixed), pl.BlockSpec((1, 1, d), grp), pl.BlockSpec((1, 1, d), grp),
                  pl.BlockSpec((d, LANES), fixed), pl.BlockSpec((1, LANES), fixed)],
        out_specs=[pl.BlockSpec((tm, d), row), pl.BlockSpec((tm, d), row),
                   pl.BlockSpec((tm, LANES), row), pl.BlockSpec((tm, LANES), row)],
        out_shape=[jax.ShapeDtypeStruct((m, d), F32), jax.ShapeDtypeStruct((m, d), F32),
                   jax.ShapeDtypeStruct((m, LANES), jnp.int32), jax.ShapeDtypeStruct((m, LANES), F32)],
        compiler_params=_cparams(("parallel",)),
        name="postmix_router",
    )(x, mix, g_post.reshape(1, d), gate1, g_pre.reshape(1, d), scale2, shift2, router_w, router_b)


def _row_copy(src_hbm, dst_vmem, sem, src_row, dst_row):
    return pltpu.make_async_copy(src_hbm.at[pl.ds(src_row, 1)], dst_vmem.at[pl.ds(dst_row, 1)], sem)


def _gather_kernel(tok_ref, nv_ref, h_hbm, o_ref, buf_ref, sem):
    i = pl.program_id(0)
    nv = nv_ref[i]

    @pl.when(nv == 0)
    def _():
        o_ref[...] = jnp.zeros(o_ref.shape, o_ref.dtype)

    @pl.when(nv > 0)
    def _():
        @pl.when(nv < ROW_CHUNK)
        def _():
            buf_ref[...] = jnp.zeros(buf_ref.shape, F32)

        def issue(r, carry):
            _row_copy(h_hbm, buf_ref, sem, tok_ref[i * ROW_CHUNK + r], r).start()
            return carry

        lax.fori_loop(0, nv, issue, 0)

        def drain(r, carry):
            _row_copy(h_hbm, buf_ref, sem, 0, r).wait()
            return carry

        lax.fori_loop(0, nv, drain, 0)
        o_ref[...] = buf_ref[...].astype(o_ref.dtype)


def _gather_rows(row_tok, chunk_valid, h, n_rows):
    d = h.shape[1]
    return pl.pallas_call(
        _gather_kernel,
        grid_spec=pltpu.PrefetchScalarGridSpec(
            num_scalar_prefetch=2,
            grid=(n_rows // ROW_CHUNK,),
            in_specs=[pl.BlockSpec(memory_space=pl.ANY)],
            out_specs=pl.BlockSpec((ROW_CHUNK, d), lambda i, tok, nv: (i, 0)),
            scratch_shapes=[pltpu.VMEM((ROW_CHUNK, d), F32), pltpu.SemaphoreType.DMA(())]),
        out_shape=jax.ShapeDtypeStruct((n_rows, d), BF16),
        compiler_params=_cparams(("arbitrary",)),
        name="moe_gather",
    )(row_tok, chunk_valid, h)


def _expert_in_kernel(sbe_ref, sbv_ref, sbx_ref, x_ref, wg_ref, wu_ref, bg_ref, bu_ref, o_ref,
                      wgb_ref, wub_ref):
    nv = sbv_ref[pl.program_id(0)]

    @pl.when(nv > 0)
    def _():
        wgb_ref[...] = wg_ref[0].astype(BF16)
        wub_ref[...] = wu_ref[0].astype(BF16)

    for c in range(EXPERT_ROWS // ROW_CHUNK):
        rows = slice(c * ROW_CHUNK, (c + 1) * ROW_CHUNK)

        @pl.when(c * ROW_CHUNK < nv)
        def _():
            xs = x_ref[rows, :]
            g = jnp.dot(xs, wgb_ref[...], preferred_element_type=F32) + bg_ref[0]
            u = jnp.dot(xs, wub_ref[...], preferred_element_type=F32) + bu_ref[0]
            g = jnp.minimum(g, SWIGLU_LIMIT)
            u = jnp.clip(u, -SWIGLU_LIMIT, SWIGLU_LIMIT)
            o_ref[rows, :] = (g * _sigmoid(SWIGLU_ALPHA * g) * (u + 1.0)).astype(o_ref.dtype)

        @pl.when(c * ROW_CHUNK >= nv)
        def _():
            o_ref[rows, :] = jnp.zeros((ROW_CHUNK, o_ref.shape[1]), o_ref.dtype)


def _expert_out_kernel(sbe_ref, sbv_ref, sbx_ref, a_ref, wd_ref, bd_ref, o_ref, wdb_ref):
    nv = sbv_ref[pl.program_id(0)]

    @pl.when(nv > 0)
    def _():
        wdb_ref[...] = wd_ref[0].astype(BF16)

    for c in range(EXPERT_ROWS // ROW_CHUNK):
        rows = slice(c * ROW_CHUNK, (c + 1) * ROW_CHUNK)

        @pl.when(c * ROW_CHUNK < nv)
        def _():
            o_ref[rows, :] = jnp.dot(a_ref[rows, :], wdb_ref[...], preferred_element_type=F32) + bd_ref[0]

        @pl.when(c * ROW_CHUNK >= nv)
        def _():
            o_ref[rows, :] = jnp.zeros((ROW_CHUNK, o_ref.shape[1]), o_ref.dtype)


def _expert_specs(d_in, tn, n_col_blocks):
    def col(j, sbv, i):
        return jnp.where(sbv[i] > 0, j, n_col_blocks - 1)

    x_spec = pl.BlockSpec((EXPERT_ROWS, d_in), lambda i, j, sbe, sbv, sbx: (sbx[i], 0))
    w_spec = pl.BlockSpec((1, d_in, tn), lambda i, j, sbe, sbv, sbx: (sbe[i], 0, col(j, sbv, i)))
    b_spec = pl.BlockSpec((1, 1, tn), lambda i, j, sbe, sbv, sbx: (sbe[i], 0, col(j, sbv, i)))
    o_spec = pl.BlockSpec((EXPERT_ROWS, tn), lambda i, j, sbe, sbv, sbx: (i, j))
    return x_spec, w_spec, b_spec, o_spec


def _expert_in(meta, xg, w_gate, w_up, b_gate, b_up, tn=256):
    n_rows, d = xg.shape
    d_e = w_gate.shape[2]
    x_spec, w_spec, b_spec, o_spec = _expert_specs(d, tn, d_e // tn)
    return pl.pallas_call(
        _expert_in_kernel,
        grid_spec=pltpu.PrefetchScalarGridSpec(
            num_scalar_prefetch=3,
            grid=(n_rows // EXPERT_ROWS, d_e // tn),
            in_specs=[x_spec, w_spec, w_spec, b_spec, b_spec],
            out_specs=o_spec,
            scratch_shapes=[pltpu.VMEM((d, tn), BF16), pltpu.VMEM((d, tn), BF16)]),
        out_shape=jax.ShapeDtypeStruct((n_rows, d_e), BF16),
        compiler_params=_cparams(("arbitrary", "arbitrary")),
        name="expert_in",
    )(*meta, xg, w_gate, w_up, b_gate, b_up)


def _expert_out(meta, act, w_down, b_down, tn=256):
    n_rows, d_e = act.shape
    d = w_down.shape[2]
    x_spec, w_spec, b_spec, o_spec = _expert_specs(d_e, tn, d // tn)
    return pl.pallas_call(
        _expert_out_kernel,
        grid_spec=pltpu.PrefetchScalarGridSpec(
            num_scalar_prefetch=3,
            grid=(n_rows // EXPERT_ROWS, d // tn),
            in_specs=[x_spec, w_spec, b_spec],
            out_specs=o_spec,
            scratch_shapes=[pltpu.VMEM((d_e, tn), BF16)]),
        out_shape=jax.ShapeDtypeStruct((n_rows, d), F32),
        compiler_params=_cparams(("arbitrary", "arbitrary")),
        name="expert_out",
    )(*meta, act, w_down, b_down)


def _combine_kernel(pos_ref, y_hbm, gates_ref, x1_ref, gpost_ref, gate2_ref, o_ref, buf_ref, sem):
    i = pl.program_id(0)
    n = COMBINE_TOKENS * TOP_K

    def issue(r, carry):
        src = pos_ref[i * n + r]
        pltpu.make_async_copy(y_hbm.at[pl.ds(src, 1)], buf_ref.at[r % TOP_K, pl.ds(r // TOP_K, 1)], sem).start()
        return carry

    lax.fori_loop(0, n, issue, 0)

    def drain(r, carry):
        pltpu.make_async_copy(y_hbm.at[pl.ds(0, 1)], buf_ref.at[r % TOP_K, pl.ds(r // TOP_K, 1)], sem).wait()
        return carry

    lax.fori_loop(0, n, drain, 0)
    gates = gates_ref[...]
    moe = gates[:, 0:1] * buf_ref[0]
    for k in range(1, TOP_K):
        moe = moe + gates[:, k:k + 1] * buf_ref[k]
    o_ref[...] = x1_ref[...] + gate2_ref[0] * (_rms_rows(moe) * gpost_ref[...])


def _combine(pos, yb, gates, x1, g_post, gate2, group_of_block):
    m, d = x1.shape
    tm = COMBINE_TOKENS
    return pl.pallas_call(
        _combine_kernel,
        grid_spec=pltpu.PrefetchScalarGridSpec(
            num_scalar_prefetch=1,
            grid=(m // tm,),
            in_specs=[pl.BlockSpec(memory_space=pl.ANY),
                      pl.BlockSpec((tm, LANES), lambda i, pos: (i, 0)),
                      pl.BlockSpec((tm, d), lambda i, pos: (i, 0)),
                      pl.BlockSpec((1, d), lambda i, pos: (0, 0)),
                      pl.BlockSpec((1, 1, d), lambda i, pos: (group_of_block(i, tm), 0, 0))],
            out_specs=pl.BlockSpec((tm, d), lambda i, pos: (i, 0)),
            scratch_shapes=[pltpu.VMEM((TOP_K, tm, d), F32), pltpu.SemaphoreType.DMA(())]),
        out_shape=jax.ShapeDtypeStruct((m, d), F32),
        compiler_params=_cparams(("arbitrary",)),
        name="moe_combine",
    )(pos, yb, gates, x1, g_post.reshape(1, d), gate2)


def _routing_tables(top_i, n_tok):
    m = n_tok * TOP_K
    n_sb = m // EXPERT_ROWS + N_EXPERTS
    n_rows = n_sb * EXPERT_ROWS
    flat_e = top_i.reshape(-1)
    onehot = (flat_e[:, None] == jnp.arange(N_EXPERTS, dtype=jnp.int32)[None, :]).astype(jnp.int32)
    csum = jnp.cumsum(onehot, axis=0)
    rank = jnp.take_along_axis(csum, flat_e[:, None], axis=1)[:, 0] - 1
    counts = csum[-1]
    pcounts = (counts + EXPERT_ROWS - 1) // EXPERT_ROWS * EXPERT_ROWS
    pend = jnp.cumsum(pcounts)
    pstart = pend - pcounts
    pos = (pstart[flat_e] + rank).astype(jnp.int32)
    row_tok = jnp.zeros((n_rows,), jnp.int32).at[pos].set(jnp.arange(m, dtype=jnp.int32) // TOP_K)
    sb_row = jnp.arange(n_sb, dtype=jnp.int32) * EXPERT_ROWS
    sb_e = jnp.minimum(jnp.searchsorted(pend, sb_row, side='right'), N_EXPERTS - 1).astype(jnp.int32)
    sb_valid = jnp.clip(counts[sb_e] - (sb_row - pstart[sb_e]), 0, EXPERT_ROWS)
    sb_valid = jnp.where(sb_row < pend[-1], sb_valid, 0).astype(jnp.int32)
    last = jnp.maximum(pend[-1] // EXPERT_ROWS - 1, 0).astype(jnp.int32)
    sb_x = jnp.minimum(jnp.arange(n_sb, dtype=jnp.int32), last)
    sb_e = sb_e[sb_x]
    chunk_off = jnp.arange(EXPERT_ROWS // ROW_CHUNK, dtype=jnp.int32) * ROW_CHUNK
    chunk_valid = jnp.clip(sb_valid[:, None] - chunk_off[None, :], 0, ROW_CHUNK).reshape(-1).astype(jnp.int32)
    return pos, row_tok, chunk_valid, (sb_e, sb_valid, sb_x), n_rows


def _rot_cols(w):
    a1, a2, b1, b2 = jnp.split(w, 4, axis=-1)
    return jnp.concatenate([-a2, a1, -b2, b1], axis=-1)


def _rope_angles(n_tok):
    rows = n_tok // GRID_W
    row = jnp.repeat(jnp.arange(rows), GRID_W).astype(F32)
    col = jnp.tile(jnp.arange(GRID_W), rows).astype(F32)
    half = QK_ROPE // 2
    inv = ROPE_THETA ** (-jnp.arange(0, half, 2, dtype=F32) / half)
    ang_r = row[:, None] * inv
    ang_c = col[:, None] * inv
    ang = jnp.concatenate([ang_r, ang_r, ang_c, ang_c], axis=-1)
    return jnp.cos(ang), jnp.sin(ang)


def kernel(x_prompt, x_sample, cache_ckv, cache_krope, state_ssm_fwd, state_ssm_bwd, c, c_ctx, w_mod, b_mod, norm_mix_pre, norm_mix_post, norm_ffn_pre, norm_ffn_post, w_in, norm_q, norm_kv, w_uq, w_uk, w_uv, conv_w, conv_b, dt_bias_fwd, dt_bias_bwd, a_log_fwd, a_log_bwd, d_skip, norm_ssm, w_out, router_w, router_b, w_gate, b_gate, w_up, b_up, w_down, b_down):
    assert w_mod.shape[0] == 1, "single-layer trunk"
    n_ctx_seq, ctx_len, d = x_prompt.shape
    n_lat_seq, lat_len, _ = x_sample.shape
    past_len = cache_ckv.shape[2]
    n_ctx = n_ctx_seq * ctx_len
    n_lat = n_lat_seq * lat_len
    n_tok = n_ctx + n_lat
    n_groups = 1 + n_lat_seq

    def group_of_block(i, tm):
        row = i * tm
        return jnp.maximum(row - n_ctx, 0) // lat_len + (row >= n_ctx).astype(jnp.int32)

    cond = jnp.concatenate([c_ctx[None, :], c, jnp.zeros((SUBLANES - n_groups, d), F32)], axis=0)
    mod = _modulation(cond, w_mod[0], b_mod[0])[:n_groups]
    shift1, scale1, gate1, shift2, scale2, gate2 = [t.reshape(n_groups, 1, d) for t in jnp.split(mod, 6, axis=-1)]

    x = jnp.concatenate([x_prompt.reshape(n_ctx, d), x_sample.reshape(n_lat, d)], axis=0)
    h = _prenorm(x, norm_mix_pre[0], scale1, shift1, group_of_block)

    o1, o2, o3, o4, o5 = Q_LORA, Q_LORA + KV_LORA, Q_LORA + KV_LORA + QK_ROPE, \
        Q_LORA + KV_LORA + QK_ROPE + SSM_WIDTH, Q_LORA + KV_LORA + QK_ROPE + SSM_WIDTH + CONV_DIM
    w_in0 = w_in[0]
    w_kr = w_in0[:, o2:o3]
    w_krdt = jnp.concatenate([w_kr, _rot_cols(w_kr), w_in0[:, o5:],
                              jnp.zeros((d, 2 * LANES - 2 * QK_ROPE - 2 * SSM_HEADS), F32)], axis=1)
    cq = _matmul(h, w_in0[:, :o1].astype(BF16), F32, 1024, Q_LORA, "in_proj_q")
    ckv = _matmul(h, w_in0[:, o1:o2].astype(BF16), F32, 1024, KV_LORA, "in_proj_kv")
    z = _matmul(h, w_in0[:, o3:o4].astype(BF16), F32, 1024, 512, "in_proj_z")
    xbc = _matmul(h, w_in0[:, o4:o5].astype(BF16), F32, 1024, 512, "in_proj_xbc")
    krdt = _matmul(h, w_krdt.astype(BF16), F32, 1024, 2 * LANES, "in_proj_krdt")

    cos, sin = _rope_angles(lat_len)
    ones = jnp.ones((n_ctx, QK_ROPE), F32)
    q_table = QK_SCALE * jnp.concatenate([
        jnp.concatenate([jnp.ones((n_ctx, QK_NOPE), F32), ones, 0.0 * ones], axis=1),
        jnp.tile(jnp.concatenate([jnp.ones((lat_len, QK_NOPE), F32), cos, sin], axis=1), (n_lat_seq, 1))], axis=0)
    k_table = jnp.concatenate([jnp.concatenate([ones, 0.0 * ones], axis=1),
                               jnp.tile(jnp.concatenate([cos, sin], axis=1), (n_lat_seq, 1))], axis=0)
    wq = w_uq[0].reshape(Q_LORA, MLA_HEADS, QK_NOPE + QK_ROPE)
    wq = jnp.concatenate([wq, _rot_cols(wq[..., QK_NOPE:])], axis=-1).reshape(Q_LORA, MLA_HEADS * QK_PAD)
    q = _q_proj(cq, norm_q[0], wq.astype(BF16), q_table)
    wk = w_uk[0].astype(BF16)
    wv = w_uv[0].astype(BF16)
    ckv_n, kn, v = _kv_proj(ckv, norm_kv[0], wk, wv, True, "kv_proj")
    _, kn_c, v_c = _kv_proj(cache_ckv[:, 0].reshape(n_lat_seq * past_len, KV_LORA), norm_kv[0], wk, wv,
                            False, "kv_proj_cache")
    kk = _krope(krdt, k_table)

    attn_ctx = _attention(q, kn, kk, v, n_ctx_seq, ctx_len, ctx_len, 0, ctx_len, ctx_len, "attn_ctx")

    def with_cache(lat, cached):
        lat = lat[n_ctx:].reshape(n_lat_seq, lat_len, -1)
        cached = cached.reshape(n_lat_seq, past_len, -1)
        return jnp.concatenate([lat, cached], axis=1).reshape(n_lat_seq * (lat_len + past_len), -1)

    ckr = cache_krope[:, 0].reshape(n_lat_seq * past_len, QK_ROPE)
    kk_c = jnp.concatenate([ckr, ckr], axis=1).astype(BF16)
    attn_lat = _attention(q, with_cache(kn, kn_c), with_cache(kk, kk_c), with_cache(v, v_c),
                          n_lat_seq, lat_len, lat_len + past_len, n_ctx, 512, 512, "attn_lat")

    dt_bias = jnp.concatenate([dt_bias_fwd[0], dt_bias_bwd[0], jnp.zeros((LANES - 2 * SSM_HEADS,), F32)])
    a_log = jnp.concatenate([a_log_fwd[0], a_log_bwd[0], jnp.zeros((LANES - 2 * SSM_HEADS,), F32)])
    dt, cs, cst = _dt_prep(krdt, dt_bias.reshape(1, LANES), a_log.reshape(1, LANES))
    d_wide = jnp.repeat(d_skip[0], SSM_HEADDIM).reshape(1, SSM_WIDTH)
    zero_state = jnp.zeros((n_ctx_seq, SSM_WIDTH, D_STATE), F32)
    ssm = []
    fins = []
    for n_seq, seq_len, row0, tc, init_f, init_b in (
            (n_ctx_seq, ctx_len, 0, 4 * LANES, zero_state, zero_state),
            (n_lat_seq, lat_len, n_ctx, 2 * LANES,
             state_ssm_fwd[:, 0].reshape(n_lat_seq, SSM_WIDTH, D_STATE),
             state_ssm_bwd[:, 0].reshape(n_lat_seq, SSM_WIDTH, D_STATE))):
        tag = "ctx" if row0 == 0 else "lat"
        act = _conv_silu(xbc, conv_w[0], conv_b[0], n_seq, seq_len, row0, tc, "conv_" + tag)
        y_f, fin_f = _ssd_scan(act, dt, cs, cst, init_f, n_seq, seq_len, row0, False, "ssd_fwd_" + tag)
        y_b, fin_b = _ssd_scan(act, dt, cs, cst, init_b, n_seq, seq_len, row0, True, "ssd_bwd_" + tag)
        ssm.append(_ssd_out(y_f, y_b, act, z, d_wide, norm_ssm[0], row0))
        fins.append((fin_f, fin_b))

    attn = jnp.concatenate([attn_ctx, attn_lat], axis=0)
    ssm = jnp.concatenate(ssm, axis=0)
    w_o = w_out[0].astype(BF16)
    mix = _matmul2(attn, ssm, w_o[:MLA_WIDTH], w_o[MLA_WIDTH:], F32, 1024, 512, "out_proj")
    rw = jnp.concatenate([router_w[0], jnp.zeros((d, LANES - N_EXPERTS), F32)], axis=1)
    rb = jnp.concatenate([router_b[0], jnp.zeros((LANES - N_EXPERTS,), F32)]).reshape(1, LANES)
    x1, h2, top_i, gates = _postmix_router(x, mix, norm_mix_post[0], gate1, norm_ffn_pre[0], scale2, shift2,
                                           rw, rb, group_of_block)

    pos, row_tok, chunk_valid, meta, n_rows = _routing_tables(top_i[:, :TOP_K], n_tok)
    xg = _gather_rows(row_tok, chunk_valid, h2, n_rows)
    act = _expert_in(meta, xg, w_gate[0], w_up[0], b_gate[0].reshape(N_EXPERTS, 1, -1),
                     b_up[0].reshape(N_EXPERTS, 1, -1))
    yb = _expert_out(meta, act, w_down[0], b_down[0].reshape(N_EXPERTS, 1, -1))
    x2 = _combine(pos, yb, gates, x1, norm_ffn_post[0], gate2, group_of_block)

    y_prompt = x2[:n_ctx].reshape(n_ctx_seq, ctx_len, d)
    y_sample = x2[n_ctx:].reshape(n_lat_seq, lat_len, d)
    new_ckv = ckv_n[:n_ctx].reshape(n_ctx_seq, 1, ctx_len, KV_LORA)
    new_kr = krdt[:n_ctx, :QK_ROPE].reshape(n_ctx_seq, 1, ctx_len, QK_ROPE)
    fin_f, fin_b = fins[0]
    new_f = fin_f.reshape(n_ctx_seq, 1, SSM_HEADS, SSM_HEADDIM, D_STATE)
    new_b = fin_b.reshape(n_ctx_seq, 1, SSM_HEADS, SSM_HEADDIM, D_STATE)
    return (y_prompt, y_sample, new_ckv, new_kr, new_f, new_b)
```

```python
import functools
import math

import jax
import jax.numpy as jnp
from jax import lax
from jax.experimental import pallas as pl
from jax.experimental.pallas import tpu as pltpu

F32 = jnp.float32
BF16 = jnp.bfloat16
HIGHEST = lax.Precision.HIGHEST

D_MODEL = 4096
GRID_W = 64
NORM_EPS = 1e-6
ROPE_THETA = 10000.0
MLA_HEADS = 16
QK_NOPE = 128
QK_ROPE = 64
V_HEAD = 128
Q_LORA = 768
KV_LORA = 512
MLA_WIDTH = MLA_HEADS * V_HEAD
QK_SCALE = (QK_NOPE + QK_ROPE) ** -0.5
SSM_HEADDIM = 64
SSM_HEADS = 32
SSM_WIDTH = SSM_HEADS * SSM_HEADDIM
SSM_GROUPS = 4
HEADS_PER_GROUP = SSM_HEADS // SSM_GROUPS
GROUP_WIDTH = SSM_WIDTH // SSM_GROUPS
D_STATE = 128
D_CONV = 5
CHUNK = 128
CONV_DIM = SSM_WIDTH + 2 * SSM_GROUPS * D_STATE
N_EXPERTS = 32
TOP_K = 4
SWIGLU_LIMIT = 7.0
SWIGLU_ALPHA = 1.702

LANES = 128
SUBLANES = 8
BF16_ROWS = 16
V7X_VMEM_LIMIT_BYTES = 56 * 1024 * 1024

QK_PAD = 2 * LANES
EXPERT_ROWS = 1024
ROW_CHUNK = 256
COMBINE_TOKENS = 64
DMA_UNROLL = 8


def _cparams(sem, vmem=V7X_VMEM_LIMIT_BYTES):
    return pltpu.CompilerParams(dimension_semantics=sem, vmem_limit_bytes=vmem)


def _sigmoid(x):
    return 1.0 / (1.0 + jnp.exp(-x))


def _rms_rows(x):
    return x * lax.rsqrt(jnp.mean(x * x, axis=-1, keepdims=True) + NORM_EPS)


def _mod_kernel(c_ref, w_ref, b_ref, o_ref):
    c = c_ref[...]
    s = (c * _sigmoid(c)).astype(BF16)
    o_ref[...] = jnp.dot(s, w_ref[...].astype(BF16), preferred_element_type=F32) + b_ref[...]


def _modulation(cond, w_mod, b_mod, tn=512):
    rows, k = cond.shape
    n = w_mod.shape[1]
    return pl.pallas_call(
        _mod_kernel,
        grid=(n // tn,),
        in_specs=[pl.BlockSpec((rows, k), lambda j: (0, 0)),
                  pl.BlockSpec((k, tn), lambda j: (0, j)),
                  pl.BlockSpec((1, tn), lambda j: (0, j))],
        out_specs=pl.BlockSpec((rows, tn), lambda j: (0, j)),
        out_shape=jax.ShapeDtypeStruct((rows, n), F32),
        compiler_params=_cparams(("parallel",)),
        name="modulation",
    )(cond, w_mod, b_mod.reshape(1, n))


def _prenorm_kernel(xa_ref, xb_ref, g_ref, sc_ref, sh_ref, o_ref, *, blocks_a):
    x = jnp.where(pl.program_id(0) < blocks_a, xa_ref[...], xb_ref[...])
    h = _rms_rows(x) * g_ref[...]
    o_ref[...] = (h * (1.0 + sc_ref[0]) + sh_ref[0]).astype(o_ref.dtype)


def _prenorm(xa, xb, gain, scale, shift, group_of_block, tm=256):
    d = xa.shape[1]
    m = xa.shape[0] + xb.shape[0]
    blocks_a = xa.shape[0] // tm
    return pl.pallas_call(
        functools.partial(_prenorm_kernel, blocks_a=blocks_a),
        grid=(m // tm,),
        in_specs=[pl.BlockSpec((tm, d), lambda i: (jnp.minimum(i, blocks_a - 1), 0)),
                  pl.BlockSpec((tm, d), lambda i: (jnp.maximum(i - blocks_a, 0), 0)),
                  pl.BlockSpec((1, d), lambda i: (0, 0)),
                  pl.BlockSpec((1, 1, d), lambda i: (group_of_block(i, tm), 0, 0)),
                  pl.BlockSpec((1, 1, d), lambda i: (group_of_block(i, tm), 0, 0))],
        out_specs=pl.BlockSpec((tm, d), lambda i: (i, 0)),
        out_shape=jax.ShapeDtypeStruct((m, d), BF16),
        compiler_params=_cparams(("parallel",)),
        name="prenorm",
    )(xa, xb, gain.reshape(1, d), scale, shift)


def _mm_kernel(a_ref, b_ref, o_ref):
    for r in range(a_ref.shape[0] // ROW_CHUNK):
        rows = slice(r * ROW_CHUNK, (r + 1) * ROW_CHUNK)
        o_ref[rows, :] = jnp.dot(a_ref[rows, :], b_ref[...], preferred_element_type=F32).astype(o_ref.dtype)


def _matmul(a, b, out_dtype, tm, tn, name):
    m, k = a.shape
    n = b.shape[1]
    return pl.pallas_call(
        _mm_kernel,
        grid=(m // tm, n // tn),
        in_specs=[pl.BlockSpec((tm, k), lambda i, j: (i, 0)),
                  pl.BlockSpec((k, tn), lambda i, j: (0, j))],
        out_specs=pl.BlockSpec((tm, tn), lambda i, j: (i, j)),
        out_shape=jax.ShapeDtypeStruct((m, n), out_dtype),
        compiler_params=_cparams(("parallel", "parallel")),
        name=name,
    )(a, b)


def _mm2_kernel(a1_ref, a2_ref, b1_ref, b2_ref, o_ref):
    for r in range(a1_ref.shape[0] // ROW_CHUNK):
        rows = slice(r * ROW_CHUNK, (r + 1) * ROW_CHUNK)
        acc = jnp.dot(a1_ref[rows, :], b1_ref[...], preferred_element_type=F32)
        acc = acc + jnp.dot(a2_ref[rows, :], b2_ref[...], preferred_element_type=F32)
        o_ref[rows, :] = acc.astype(o_ref.dtype)


def _matmul2(a1, a2, b1, b2, out_dtype, tm, tn, name):
    m, k1 = a1.shape
    k2 = a2.shape[1]
    n = b1.shape[1]
    return pl.pallas_call(
        _mm2_kernel,
        grid=(m // tm, n // tn),
        in_specs=[pl.BlockSpec((tm, k1), lambda i, j: (i, 0)),
                  pl.BlockSpec((tm, k2), lambda i, j: (i, 0)),
                  pl.BlockSpec((k1, tn), lambda i, j: (0, j)),
                  pl.BlockSpec((k2, tn), lambda i, j: (0, j))],
        out_specs=pl.BlockSpec((tm, tn), lambda i, j: (i, j)),
        out_shape=jax.ShapeDtypeStruct((m, n), out_dtype),
        compiler_params=_cparams(("parallel", "parallel")),
        name=name,
    )(a1, a2, b1, b2)


def _q_kernel(cq_ref, g_ref, w_ref, tab_ref, o_ref):
    xn = (_rms_rows(cq_ref[...]) * g_ref[...]).astype(BF16)
    tab = tab_ref[...]
    for h in range(MLA_HEADS):
        sl = slice(h * QK_PAD, (h + 1) * QK_PAD)
        q = jnp.dot(xn, w_ref[:, sl], preferred_element_type=F32)
        o_ref[:, sl] = (q * tab).astype(o_ref.dtype)


def _q_proj(cq, norm_q, w_q, q_table, tm=512):
    m, k = cq.shape
    n = w_q.shape[1]
    return pl.pallas_call(
        _q_kernel,
        grid=(m // tm,),
        in_specs=[pl.BlockSpec((tm, k), lambda i: (i, 0)),
                  pl.BlockSpec((1, k), lambda i: (0, 0)),
                  pl.BlockSpec((k, n), lambda i: (0, 0)),
                  pl.BlockSpec((tm, QK_PAD), lambda i: (i, 0))],
        out_specs=pl.BlockSpec((tm, n), lambda i: (i, 0)),
        out_shape=jax.ShapeDtypeStruct((m, n), BF16),
        compiler_params=_cparams(("parallel",)),
        name="q_proj",
    )(cq, norm_q.reshape(1, k), w_q, q_table)


def _kv_kernel(ckv_ref, g_ref, wk_ref, wv_ref, kk_ref, n_ref, k_ref, v_ref, *, normalise):
    x = ckv_ref[...]
    if normalise:
        x = _rms_rows(x) * g_ref[...]
    n_ref[...] = x
    xb = x.astype(BF16)
    kk = kk_ref[...]
    for h in range(MLA_HEADS):
        ws = slice(h * QK_NOPE, (h + 1) * QK_NOPE)
        k_ref[:, h * QK_PAD:h * QK_PAD + QK_NOPE] = jnp.dot(
            xb, wk_ref[:, ws], preferred_element_type=F32).astype(k_ref.dtype)
        k_ref[:, h * QK_PAD + QK_NOPE:(h + 1) * QK_PAD] = kk
    v_ref[...] = jnp.dot(xb, wv_ref[...], preferred_element_type=F32).astype(v_ref.dtype)


def _kv_proj(ckv, norm_kv, w_k, w_v, kk, normalise, name, tm=512):
    m, k = ckv.shape
    n = w_k.shape[1]
    return pl.pallas_call(
        functools.partial(_kv_kernel, normalise=normalise),
        grid=(m // tm,),
        in_specs=[pl.BlockSpec((tm, k), lambda i: (i, 0)),
                  pl.BlockSpec((1, k), lambda i: (0, 0)),
                  pl.BlockSpec((k, n), lambda i: (0, 0)),
                  pl.BlockSpec((k, n), lambda i: (0, 0)),
                  pl.BlockSpec((tm, LANES), lambda i: (i, 0))],
        out_specs=[pl.BlockSpec((tm, k), lambda i: (i, 0)),
                   pl.BlockSpec((tm, MLA_HEADS * QK_PAD), lambda i: (i, 0)),
                   pl.BlockSpec((tm, n), lambda i: (i, 0))],
        out_shape=[jax.ShapeDtypeStruct((m, k), F32),
                   jax.ShapeDtypeStruct((m, MLA_HEADS * QK_PAD), BF16),
                   jax.ShapeDtypeStruct((m, n), BF16)],
        compiler_params=_cparams(("parallel",)),
        name=name,
    )(ckv, norm_kv.reshape(1, k), w_k, w_v, kk)


def _krope_kernel(kr_ref, tab_ref, o_ref):
    t = kr_ref[...] * tab_ref[...]
    o_ref[...] = (t + pltpu.roll(t, QK_ROPE, 1)).astype(o_ref.dtype)


def _krope(krdt, k_table, tm=1024):
    m = krdt.shape[0]
    return pl.pallas_call(
        _krope_kernel,
        grid=(m // tm,),
        in_specs=[pl.BlockSpec((tm, LANES), lambda i: (i, 0)),
                  pl.BlockSpec((tm, LANES), lambda i: (i, 0))],
        out_specs=pl.BlockSpec((tm, LANES), lambda i: (i, 0)),
        out_shape=jax.ShapeDtypeStruct((m, LANES), BF16),
        compiler_params=_cparams(("parallel",)),
        name="krope",
    )(krdt, k_table)


def _attn_kernel(q_ref, k_ref, v_ref, o_ref, *, heads):
    for h in range(heads):
        q = q_ref[:, h * QK_PAD:(h + 1) * QK_PAD]
        k = k_ref[:, h * QK_PAD:(h + 1) * QK_PAD]
        s = lax.dot_general(q, k, (((1,), (1,)), ((), ())), preferred_element_type=F32)
        p = jnp.exp2(s - jnp.max(s, axis=1, keepdims=True))
        denom = jnp.sum(p, axis=1, keepdims=True)
        o = jnp.dot(p.astype(BF16), v_ref[:, h * V_HEAD:(h + 1) * V_HEAD], preferred_element_type=F32)
        o_ref[:, h * V_HEAD:(h + 1) * V_HEAD] = (o / denom).astype(o_ref.dtype)


def _attention(q, k, v, n_batch, s_len, kv_len, q_row0, tq, heads, name):
    nq = s_len // tq
    q_blk0 = q_row0 // tq
    return pl.pallas_call(
        functools.partial(_attn_kernel, heads=heads),
        grid=(n_batch, MLA_HEADS // heads, nq),
        in_specs=[pl.BlockSpec((tq, heads * QK_PAD), lambda b, h, qi: (q_blk0 + b * nq + qi, h)),
                  pl.BlockSpec((kv_len, heads * QK_PAD), lambda b, h, qi: (b, h)),
                  pl.BlockSpec((kv_len, heads * V_HEAD), lambda b, h, qi: (b, h))],
        out_specs=pl.BlockSpec((tq, heads * V_HEAD), lambda b, h, qi: (b * nq + qi, h)),
        out_shape=jax.ShapeDtypeStruct((n_batch * s_len, MLA_WIDTH), BF16),
        compiler_params=_cparams(("parallel", "parallel", "parallel")),
        name=name,
    )(q, k, v)


def _conv_kernel(u_ref, w_ref, b_ref, o_ref, pad_ref, *, seq_len):
    halo = SUBLANES
    width = u_ref.shape[-1]
    pad_ref[0:halo, :] = jnp.zeros((halo, width), F32)
    pad_ref[halo + seq_len:2 * halo + seq_len, :] = jnp.zeros((halo, width), F32)
    pad_ref[halo:halo + seq_len, :] = u_ref[...]
    for r in range(seq_len // ROW_CHUNK):
        acc = b_ref[...]
        for k in range(D_CONV):
            start = halo - D_CONV // 2 + k + r * ROW_CHUNK
            acc = acc + w_ref[k:k + 1, :] * pad_ref[start:start + ROW_CHUNK, :]
        o_ref[r * ROW_CHUNK:(r + 1) * ROW_CHUNK, :] = acc * _sigmoid(acc)


def _conv_silu(xbc, conv_w, conv_b, n_seq, seq_len, row0, tc, name):
    width = xbc.shape[1]
    blk0 = row0 // seq_len
    return pl.pallas_call(
        functools.partial(_conv_kernel, seq_len=seq_len),
        grid=(n_seq, width // tc),
        in_specs=[pl.BlockSpec((seq_len, tc), lambda s, c: (blk0 + s, c)),
                  pl.BlockSpec((D_CONV, tc), lambda s, c: (0, c)),
                  pl.BlockSpec((1, tc), lambda s, c: (0, c))],
        out_specs=pl.BlockSpec((seq_len, tc), lambda s, c: (s, c)),
        out_shape=jax.ShapeDtypeStruct((n_seq * seq_len, width), F32),
        scratch_shapes=[pltpu.VMEM((seq_len + 2 * SUBLANES, tc), F32)],
        compiler_params=_cparams(("parallel", "parallel")),
        name=name,
    )(xbc, conv_w, conv_b.reshape(1, width))


def _dt_kernel(raw_ref, bias_ref, alog_ref, dtf_ref, csf_ref, cstf_ref, dtb_ref, csb_ref, cstb_ref):
    raw = raw_ref[...] + bias_ref[...]
    dt = jnp.maximum(raw, 0.0) + jnp.log1p(jnp.exp(-jnp.abs(raw)))
    a = dt * (-jnp.exp(alog_ref[...]))
    row = lax.broadcasted_iota(jnp.int32, (CHUNK, CHUNK), 0)
    col = lax.broadcasted_iota(jnp.int32, (CHUNK, CHUNK), 1)
    lower = (row >= col).astype(F32)
    upper = (row <= col).astype(F32)
    prefix = jnp.dot(lower, a, precision=HIGHEST, preferred_element_type=F32)
    suffix = jnp.dot(upper, a, precision=HIGHEST, preferred_element_type=F32)
    dtf_ref[...] = dt
    csf_ref[...] = prefix
    cstf_ref[...] = prefix.T
    suffix_b = pltpu.roll(suffix, LANES - SSM_HEADS, 1)
    dtb_ref[...] = pltpu.roll(dt, LANES - SSM_HEADS, 1)
    csb_ref[...] = suffix_b
    cstb_ref[...] = suffix_b.T


def _dt_prep(krdt, dt_bias, a_log):
    m = krdt.shape[0]
    shp = jax.ShapeDtypeStruct((m, LANES), F32)
    spec = pl.BlockSpec((CHUNK, LANES), lambda i: (i, 0))
    return pl.pallas_call(
        _dt_kernel,
        grid=(m // CHUNK,),
        in_specs=[pl.BlockSpec((CHUNK, LANES), lambda i: (i, 1)),
                  pl.BlockSpec((1, LANES), lambda i: (0, 0)),
                  pl.BlockSpec((1, LANES), lambda i: (0, 0))],
        out_specs=[spec] * 6,
        out_shape=[shp] * 6,
        compiler_params=_cparams(("parallel",)),
        name="dt_prep",
    )(krdt, dt_bias, a_log)


def _replicate_lanes(x, e_ref):
    hi = x.astype(BF16)
    rest = x - hi.astype(F32)
    mid = rest.astype(BF16)
    lo = (rest - mid.astype(F32)).astype(BF16)
    e = e_ref[...]
    out = jnp.dot(hi, e, preferred_element_type=F32)
    out = out + jnp.dot(mid, e, preferred_element_type=F32)
    return out + jnp.dot(lo, e, preferred_element_type=F32)


def _ssd_direction(x_ref, b_ref, c_ref, dt_ref, cs_ref, cst_ref, y_ref, st_ref, e_head_ref, e_lane_ref,
                   reverse):
    ii = lax.broadcasted_iota(jnp.int32, (CHUNK, CHUNK), 0)
    jj = lax.broadcasted_iota(jnp.int32, (CHUNK, CHUNK), 1)
    mask = (jj >= ii) if reverse else (ii >= jj)
    cs = cs_ref[...]
    cst = cst_ref[...]
    edge = cs[0:BF16_ROWS, :] if reverse else cs[CHUNK - BF16_ROWS:CHUNK, :]
    wide = _replicate_lanes(jnp.concatenate([dt_ref[...], cs, edge], axis=0), e_head_ref)
    dt_w = wide[0:CHUNK]
    p_w = wide[CHUNK:2 * CHUNK]
    edge_row = 2 * CHUNK + (0 if reverse else BF16_ROWS - 1)
    tot_w = wide[edge_row:edge_row + 1]
    p_rep = _replicate_lanes(cs, e_lane_ref)
    xdt = x_ref[...] * dt_w
    xdec = (xdt * jnp.exp(tot_w - p_w)).astype(BF16)
    scale_in = jnp.exp(p_w)
    decay_tot = jnp.exp(tot_w)
    lane_w = lax.broadcasted_iota(jnp.int32, xdt.shape, 1)
    first_of_pair = (lane_w & SSM_HEADDIM) == 0
    x_first = jnp.where(first_of_pair, xdt, 0.0).astype(BF16)
    x_second = jnp.where(first_of_pair, 0.0, xdt).astype(BF16)
    for g in range(SSM_GROUPS):
        gs = slice(g * D_STATE, (g + 1) * D_STATE)
        gw = slice(g * GROUP_WIDTH, (g + 1) * GROUP_WIDTH)
        b_f32 = b_ref[:, gs]
        bg = b_f32.astype(BF16)
        bgt = b_f32.T.astype(BF16)
        cg = c_ref[:, gs].astype(BF16)
        cb = lax.dot_general(cg, bg, (((1,), (1,)), ((), ())), preferred_element_type=F32)
        st_old = st_ref[:, gw]
        y_off = jnp.dot(cg, st_old.astype(BF16), preferred_element_type=F32)
        st_ref[:, gw] = st_old * decay_tot[:, gw] + jnp.dot(bgt, xdec[:, gw], preferred_element_type=F32)
        for pair in range(HEADS_PER_GROUP // 2):
            h0 = g * HEADS_PER_GROUP + 2 * pair
            ps = slice(h0 * SSM_HEADDIM, (h0 + 2) * SSM_HEADDIM)
            acc = None
            for h, xm in ((h0, x_first), (h0 + 1, x_second)):
                w = jnp.where(mask, jnp.exp(p_rep[:, h * LANES:(h + 1) * LANES] - cst[h:h + 1, :]), 0.0)
                term = jnp.dot((cb * w).astype(BF16), xm[:, ps], preferred_element_type=F32)
                acc = term if acc is None else acc + term
            y_ref[:, ps] = acc + y_off[:, 2 * pair * SSM_HEADDIM:(2 * pair + 2) * SSM_HEADDIM] * scale_in[:, ps]


def _ssd_kernel(xf_ref, bf_ref, cf_ref, dtf_ref, csf_ref, cstf_ref, initf_ref,
                xb_ref, bb_ref, cb_ref, dtb_ref, csb_ref, cstb_ref, initb_ref,
                yf_ref, yb_ref, finf_ref, finb_ref, stf_ref, stb_ref, e_head_ref, e_lane_ref, *, n_chunks):
    c = pl.program_id(1)

    @pl.when(c == 0)
    def _():
        stf_ref[...] = initf_ref[0].T
        stb_ref[...] = initb_ref[0].T
        row = lax.broadcasted_iota(jnp.int32, e_head_ref.shape, 0)
        col = lax.broadcasted_iota(jnp.int32, e_head_ref.shape, 1)
        e_head_ref[...] = jnp.where((col >> (SSM_HEADDIM.bit_length() - 1)) == row, 1.0, 0.0).astype(BF16)
        row = lax.broadcasted_iota(jnp.int32, e_lane_ref.shape, 0)
        col = lax.broadcasted_iota(jnp.int32, e_lane_ref.shape, 1)
        e_lane_ref[...] = jnp.where((col >> (LANES.bit_length() - 1)) == row, 1.0, 0.0).astype(BF16)

    _ssd_direction(xf_ref, bf_ref, cf_ref, dtf_ref, csf_ref, cstf_ref, yf_ref, stf_ref,
                   e_head_ref, e_lane_ref, False)
    _ssd_direction(xb_ref, bb_ref, cb_ref, dtb_ref, csb_ref, cstb_ref, yb_ref, stb_ref,
                   e_head_ref, e_lane_ref, True)

    @pl.when(c == n_chunks - 1)
    def _():
        finf_ref[0] = stf_ref[...].T
        finb_ref[0] = stb_ref[...].T


def _ssd_scan(xbc, dt_arrays, init_f, init_b, n_seq, seq_len, row0, name):
    n_chunks = seq_len // CHUNK
    blk0 = row0 // CHUNK
    xblk = SSM_WIDTH // (SSM_GROUPS * D_STATE)
    dt_f, cs_f, cst_f, dt_b, cs_b, cst_b = dt_arrays

    def direction_specs(chunk):
        small = pl.BlockSpec((CHUNK, LANES), lambda s, c: (blk0 + chunk(s, c), 0))
        return [pl.BlockSpec((CHUNK, SSM_WIDTH), lambda s, c: (chunk(s, c), 0)),
                pl.BlockSpec((CHUNK, SSM_GROUPS * D_STATE), lambda s, c: (chunk(s, c), xblk)),
                pl.BlockSpec((CHUNK, SSM_GROUPS * D_STATE), lambda s, c: (chunk(s, c), xblk + 1)),
                small, small, small,
                pl.BlockSpec((1, SSM_WIDTH, D_STATE), lambda s, c: (s, 0, 0))]

    fwd = lambda s, c: s * n_chunks + c
    bwd = lambda s, c: s * n_chunks + (n_chunks - 1 - c)
    state_spec = pl.BlockSpec((1, SSM_WIDTH, D_STATE), lambda s, c: (s, 0, 0))
    y_shape = jax.ShapeDtypeStruct((n_seq * seq_len, SSM_WIDTH), F32)
    state_shape = jax.ShapeDtypeStruct((n_seq, SSM_WIDTH, D_STATE), F32)
    return pl.pallas_call(
        functools.partial(_ssd_kernel, n_chunks=n_chunks),
        grid=(n_seq, n_chunks),
        in_specs=direction_specs(fwd) + direction_specs(bwd),
        out_specs=[pl.BlockSpec((CHUNK, SSM_WIDTH), lambda s, c: (fwd(s, c), 0)),
                   pl.BlockSpec((CHUNK, SSM_WIDTH), lambda s, c: (bwd(s, c), 0)),
                   state_spec, state_spec],
        out_shape=[y_shape, y_shape, state_shape, state_shape],
        scratch_shapes=[pltpu.VMEM((D_STATE, SSM_WIDTH), F32), pltpu.VMEM((D_STATE, SSM_WIDTH), F32),
                        pltpu.VMEM((LANES, SSM_WIDTH), BF16), pltpu.VMEM((LANES, SSM_HEADS * LANES), BF16)],
        compiler_params=_cparams(("parallel", "arbitrary")),
        name=name,
    )(xbc, xbc, xbc, dt_f, cs_f, cst_f, init_f, xbc, xbc, xbc, dt_b, cs_b, cst_b, init_b)


def _ssd_out_kernel(yf_ref, yb_ref, x_ref, z_ref, d_ref, g_ref, o_ref):
    for g in range(SSM_GROUPS):
        gs = slice(g * GROUP_WIDTH, (g + 1) * GROUP_WIDTH)
        z = z_ref[:, gs]
        y = (yf_ref[:, gs] + yb_ref[:, gs] + d_ref[:, gs] * x_ref[:, gs]) * (z * _sigmoid(z))
        o_ref[:, gs] = (_rms_rows(y) * g_ref[:, gs]).astype(o_ref.dtype)


def _ssd_out(y_f, y_b, xbc, z, d_wide, norm_ssm, row0, tm=256):
    m = y_f.shape[0]
    blk0 = row0 // tm
    return pl.pallas_call(
        _ssd_out_kernel,
        grid=(m // tm,),
        in_specs=[pl.BlockSpec((tm, SSM_WIDTH), lambda i: (i, 0)),
                  pl.BlockSpec((tm, SSM_WIDTH), lambda i: (i, 0)),
                  pl.BlockSpec((tm, SSM_WIDTH), lambda i: (i, 0)),
                  pl.BlockSpec((tm, SSM_WIDTH), lambda i: (blk0 + i, 0)),
                  pl.BlockSpec((1, SSM_WIDTH), lambda i: (0, 0)),
                  pl.BlockSpec((1, SSM_WIDTH), lambda i: (0, 0))],
        out_specs=pl.BlockSpec((tm, SSM_WIDTH), lambda i: (i, 0)),
        out_shape=jax.ShapeDtypeStruct((m, SSM_WIDTH), BF16),
        compiler_params=_cparams(("parallel",)),
        name="ssd_out",
    )(y_f, y_b, xbc, z, d_wide, norm_ssm.reshape(1, SSM_WIDTH))


def _postmix_kernel(xa_ref, xb_ref, mix_ref, gpost_ref, gate1_ref, gpre_ref, sc_ref, sh_ref, rw_ref, rb_ref,
                    x1_ref, h_ref, idx_ref, gates_ref, rank_ref, counts_ref, run_ref, *, blocks_a):
    i = pl.program_id(0)

    @pl.when(i == 0)
    def _():
        run_ref[...] = jnp.zeros(run_ref.shape, F32)

    x = jnp.where(i < blocks_a, xa_ref[...], xb_ref[...])
    x1 = x + gate1_ref[0] * (_rms_rows(mix_ref[...]) * gpost_ref[...])
    x1_ref[...] = x1
    h = (_rms_rows(x1) * gpre_ref[...]) * (1.0 + sc_ref[0]) + sh_ref[0]
    h_ref[...] = h
    logits = jnp.dot(h, rw_ref[...], precision=HIGHEST, preferred_element_type=F32) + rb_ref[...]
    lane = lax.broadcasted_iota(jnp.int32, logits.shape, 1)
    lane_f = lane.astype(F32)
    live = jnp.where(lane < N_EXPERTS, logits, -jnp.inf)
    vals, idxs = [], []
    for _ in range(TOP_K):
        top = jnp.max(live, axis=1, keepdims=True)
        first = jnp.min(jnp.where(live == top, lane_f, float(LANES)), axis=1, keepdims=True)
        vals.append(top)
        idxs.append(first)
        live = jnp.where(lane_f == first, -jnp.inf, live)
    exps = [jnp.exp(v - vals[0]) for v in vals]
    denom = exps[0]
    for e in exps[1:]:
        denom = denom + e
    gate_out = jnp.zeros(logits.shape, F32)
    idx_out = jnp.zeros(logits.shape, F32)
    for k in range(TOP_K):
        gate_out = jnp.where(lane == k, exps[k] / denom, gate_out)
        idx_out = jnp.where(lane == k, idxs[k], idx_out)
    gates_ref[...] = gate_out
    idx_ref[...] = idx_out.astype(jnp.int32)

    hots = [jnp.where(lane_f == idxs[k], 1.0, 0.0) for k in range(TOP_K)]
    multi = hots[0]
    for hot in hots[1:]:
        multi = multi + hot
    n_rows = multi.shape[0]
    earlier = (lax.broadcasted_iota(jnp.int32, (n_rows, n_rows), 0)
               > lax.broadcasted_iota(jnp.int32, (n_rows, n_rows), 1))
    before = run_ref[0:1, :] + jnp.dot(jnp.where(earlier, 1.0, 0.0).astype(BF16), multi.astype(BF16),
                                       preferred_element_type=F32)
    rank_out = jnp.zeros(logits.shape, F32)
    for k in range(TOP_K):
        rank_out = jnp.where(lane == k, jnp.sum(hots[k] * before, axis=1, keepdims=True), rank_out)
    rank_ref[...] = rank_out.astype(jnp.int32)
    run_ref[...] = run_ref[...] + jnp.sum(multi, axis=0, keepdims=True)
    counts_ref[...] = run_ref[...].astype(jnp.int32)


def _postmix_router(xa, xb, mix, g_post, gate1, g_pre, scale2, shift2, router_w, router_b,
                    group_of_block, tm=128):
    d = xa.shape[1]
    m = xa.shape[0] + xb.shape[0]
    blocks_a = xa.shape[0] // tm
    row = lambda i: (i, 0)
    fixed = lambda i: (0, 0)
    grp = lambda i: (group_of_block(i, tm), 0, 0)
    wide = jax.ShapeDtypeStruct((m, d), F32)
    narrow_i = jax.ShapeDtypeStruct((m, LANES), jnp.int32)
    return pl.pallas_call(
        functools.partial(_postmix_kernel, blocks_a=blocks_a),
        grid=(m // tm,),
        in_specs=[pl.BlockSpec((tm, d), lambda i: (jnp.minimum(i, blocks_a - 1), 0)),
                  pl.BlockSpec((tm, d), lambda i: (jnp.maximum(i - blocks_a, 0), 0)),
                  pl.BlockSpec((tm, d), row),
                  pl.BlockSpec((1, d), fixed), pl.BlockSpec((1, 1, d), grp),
                  pl.BlockSpec((1, d), fixed), pl.BlockSpec((1, 1, d), grp), pl.BlockSpec((1, 1, d), grp),
                  pl.BlockSpec((d, LANES), fixed), pl.BlockSpec((1, LANES), fixed)],
        out_specs=[pl.BlockSpec((tm, d), row), pl.BlockSpec((tm, d), row),
                   pl.BlockSpec((tm, LANES), row), pl.BlockSpec((tm, LANES), row),
                   pl.BlockSpec((tm, LANES), row), pl.BlockSpec((SUBLANES, LANES), fixed)],
        out_shape=[wide, wide, narrow_i, jax.ShapeDtypeStruct((m, LANES), F32), narrow_i,
                   jax.ShapeDtypeStruct((SUBLANES, LANES), jnp.int32)],
        scratch_shapes=[pltpu.VMEM((SUBLANES, LANES), F32)],
        compiler_params=_cparams(("arbitrary",)),
        name="postmix_router",
    )(xa, xb, mix, g_post.reshape(1, d), gate1, g_pre.reshape(1, d), scale2, shift2, router_w, router_b)


def _row_copy(src_hbm, dst_vmem, sem, src_row, dst_row):
    return pltpu.make_async_copy(src_hbm.at[pl.ds(src_row, 1)], dst_vmem.at[pl.ds(dst_row, 1)], sem)


def _gather_kernel(tok_ref, nv_ref, h_hbm, o_ref, buf_ref, sem):
    i = pl.program_id(0)

    def fetch(blk, slot):
        n_live = nv_ref[blk]

        def issue(r, carry):
            _row_copy(h_hbm, buf_ref.at[slot], sem.at[slot], tok_ref[blk * ROW_CHUNK + r], r).start()
            return carry

        @pl.when(n_live == ROW_CHUNK)
        def _():
            lax.fori_loop(0, ROW_CHUNK, issue, 0, unroll=DMA_UNROLL)

        @pl.when((n_live > 0) & (n_live < ROW_CHUNK))
        def _():
            buf_ref[slot] = jnp.zeros(buf_ref.shape[1:], F32)
            lax.fori_loop(0, n_live, issue, 0)

    @pl.when(i == 0)
    def _():
        fetch(0, 0)

    slot = lax.rem(i, 2)

    @pl.when(i + 1 < pl.num_programs(0))
    def _():
        fetch(i + 1, 1 - slot)

    nv = nv_ref[i]

    @pl.when(nv == ROW_CHUNK)
    def _():
        pltpu.make_async_copy(h_hbm.at[pl.ds(0, ROW_CHUNK)], buf_ref.at[slot], sem.at[slot]).wait()

    @pl.when((nv > 0) & (nv < ROW_CHUNK))
    def _():
        def drain(r, carry):
            _row_copy(h_hbm, buf_ref.at[slot], sem.at[slot], 0, r).wait()
            return carry

        lax.fori_loop(0, nv, drain, 0)

    @pl.when(nv > 0)
    def _():
        o_ref[...] = buf_ref[slot].astype(o_ref.dtype)

    @pl.when(nv == 0)
    def _():
        o_ref[...] = jnp.zeros(o_ref.shape, o_ref.dtype)


def _gather_rows(row_tok, chunk_valid, h, n_rows):
    d = h.shape[1]
    return pl.pallas_call(
        _gather_kernel,
        grid_spec=pltpu.PrefetchScalarGridSpec(
            num_scalar_prefetch=2,
            grid=(n_rows // ROW_CHUNK,),
            in_specs=[pl.BlockSpec(memory_space=pl.ANY)],
            out_specs=pl.BlockSpec((ROW_CHUNK, d), lambda i, tok, nv: (i, 0)),
            scratch_shapes=[pltpu.VMEM((2, ROW_CHUNK, d), F32), pltpu.SemaphoreType.DMA((2,))]),
        out_shape=jax.ShapeDtypeStruct((n_rows, d), BF16),
        compiler_params=_cparams(("arbitrary",)),
        name="moe_gather",
    )(row_tok, chunk_valid, h)


def _over_live_rows(nv, compute, o_ref):
    def zero(r0, n):
        o_ref[r0:r0 + n, :] = jnp.zeros((n, o_ref.shape[1]), o_ref.dtype)

    for p in range(EXPERT_ROWS // (2 * ROW_CHUNK)):
        r0 = p * 2 * ROW_CHUNK

        @pl.when(nv > r0 + ROW_CHUNK)
        def _():
            compute(r0, 2 * ROW_CHUNK)

        @pl.when((nv > r0) & (nv <= r0 + ROW_CHUNK))
        def _():
            compute(r0, ROW_CHUNK)
            zero(r0 + ROW_CHUNK, ROW_CHUNK)

        @pl.when(nv <= r0)
        def _():
            zero(r0, 2 * ROW_CHUNK)


def _expert_in_kernel(sbe_ref, sbv_ref, sbx_ref, x_ref, wg_ref, wu_ref, bg_ref, bu_ref, o_ref,
                      wgb_ref, wub_ref):
    nv = sbv_ref[pl.program_id(0)]

    @pl.when(nv > 0)
    def _():
        wgb_ref[...] = wg_ref[0].astype(BF16)
        wub_ref[...] = wu_ref[0].astype(BF16)

    def compute(r0, n):
        rows = slice(r0, r0 + n)
        xs = x_ref[rows, :]
        g = jnp.dot(xs, wgb_ref[...], preferred_element_type=F32) + bg_ref[0]
        u = jnp.dot(xs, wub_ref[...], preferred_element_type=F32) + bu_ref[0]
        g = jnp.minimum(g, SWIGLU_LIMIT)
        u = jnp.clip(u, -SWIGLU_LIMIT, SWIGLU_LIMIT)
        o_ref[rows, :] = (g * _sigmoid(SWIGLU_ALPHA * g) * (u + 1.0)).astype(o_ref.dtype)

    _over_live_rows(nv, compute, o_ref)


def _expert_out_kernel(sbe_ref, sbv_ref, sbx_ref, a_ref, wd_ref, bd_ref, o_ref, wdb_ref):
    nv = sbv_ref[pl.program_id(0)]

    @pl.when(nv > 0)
    def _():
        wdb_ref[...] = wd_ref[0].astype(BF16)

    def compute(r0, n):
        rows = slice(r0, r0 + n)
        o_ref[rows, :] = jnp.dot(a_ref[rows, :], wdb_ref[...], preferred_element_type=F32) + bd_ref[0]

    _over_live_rows(nv, compute, o_ref)


def _expert_specs(d_in, tn, n_col_blocks):
    def col(j, sbv, i):
        return jnp.where(sbv[i] > 0, j, n_col_blocks - 1)

    x_spec = pl.BlockSpec((EXPERT_ROWS, d_in), lambda i, j, sbe, sbv, sbx: (sbx[i], 0))
    w_spec = pl.BlockSpec((1, d_in, tn), lambda i, j, sbe, sbv, sbx: (sbe[i], 0, col(j, sbv, i)))
    b_spec = pl.BlockSpec((1, 1, tn), lambda i, j, sbe, sbv, sbx: (sbe[i], 0, col(j, sbv, i)))
    o_spec = pl.BlockSpec((EXPERT_ROWS, tn), lambda i, j, sbe, sbv, sbx: (i, j))
    return x_spec, w_spec, b_spec, o_spec


def _expert_in(meta, xg, w_gate, w_up, b_gate, b_up, tn=256):
    n_rows, d = xg.shape
    d_e = w_gate.shape[2]
    x_spec, w_spec, b_spec, o_spec = _expert_specs(d, tn, d_e // tn)
    return pl.pallas_call(
        _expert_in_kernel,
        grid_spec=pltpu.PrefetchScalarGridSpec(
            num_scalar_prefetch=3,
            grid=(n_rows // EXPERT_ROWS, d_e // tn),
            in_specs=[x_spec, w_spec, w_spec, b_spec, b_spec],
            out_specs=o_spec,
            scratch_shapes=[pltpu.VMEM((d, tn), BF16), pltpu.VMEM((d, tn), BF16)]),
        out_shape=jax.ShapeDtypeStruct((n_rows, d_e), BF16),
        compiler_params=_cparams(("arbitrary", "arbitrary")),
        name="expert_in",
    )(*meta, xg, w_gate, w_up, b_gate, b_up)


def _expert_out(meta, act, w_down, b_down, tn=512):
    n_rows, d_e = act.shape
    d = w_down.shape[2]
    x_spec, w_spec, b_spec, o_spec = _expert_specs(d_e, tn, d // tn)
    return pl.pallas_call(
        _expert_out_kernel,
        grid_spec=pltpu.PrefetchScalarGridSpec(
            num_scalar_prefetch=3,
            grid=(n_rows // EXPERT_ROWS, d // tn),
            in_specs=[x_spec, w_spec, b_spec],
            out_specs=o_spec,
            scratch_shapes=[pltpu.VMEM((d_e, tn), BF16)]),
        out_shape=jax.ShapeDtypeStruct((n_rows, d), F32),
        compiler_params=_cparams(("arbitrary", "arbitrary")),
        name="expert_out",
    )(*meta, act, w_down, b_down)


def _combine_kernel(pos_ref, y_hbm, gates_ref, x1_ref, gpost_ref, gate2_ref, oa_ref, ob_ref, buf_ref, sem,
                    *, blocks_a):
    i = pl.program_id(0)
    n = COMBINE_TOKENS * TOP_K

    def fetch(blk, slot):
        def issue(r, carry):
            _row_copy(y_hbm, buf_ref.at[slot], sem.at[slot], pos_ref[blk * n + r], r).start()
            return carry

        lax.fori_loop(0, n, issue, 0, unroll=DMA_UNROLL)

    @pl.when(i == 0)
    def _():
        fetch(0, 0)

    slot = lax.rem(i, 2)

    @pl.when(i + 1 < pl.num_programs(0))
    def _():
        fetch(i + 1, 1 - slot)

    pltpu.make_async_copy(y_hbm.at[pl.ds(0, n)], buf_ref.at[slot], sem.at[slot]).wait()
    gates = gates_ref[...]
    moe = gates[:, 0:1] * buf_ref[slot, 0:COMBINE_TOKENS]
    for k in range(1, TOP_K):
        moe = moe + gates[:, k:k + 1] * buf_ref[slot, k * COMBINE_TOKENS:(k + 1) * COMBINE_TOKENS]
    out = x1_ref[...] + gate2_ref[0] * (_rms_rows(moe) * gpost_ref[...])

    @pl.when(i < blocks_a)
    def _():
        oa_ref[...] = out

    @pl.when(i >= blocks_a)
    def _():
        ob_ref[...] = out


def _combine(pos, yb, gates, x1, g_post, gate2, group_of_block, rows_a):
    m, d = x1.shape
    tm = COMBINE_TOKENS
    blocks_a = rows_a // tm
    pos_blocked = pos.reshape(m // tm, tm, TOP_K).transpose(0, 2, 1).reshape(-1)
    return pl.pallas_call(
        functools.partial(_combine_kernel, blocks_a=blocks_a),
        grid_spec=pltpu.PrefetchScalarGridSpec(
            num_scalar_prefetch=1,
            grid=(m // tm,),
            in_specs=[pl.BlockSpec(memory_space=pl.ANY),
                      pl.BlockSpec((tm, LANES), lambda i, pos: (i, 0)),
                      pl.BlockSpec((tm, d), lambda i, pos: (i, 0)),
                      pl.BlockSpec((1, d), lambda i, pos: (0, 0)),
                      pl.BlockSpec((1, 1, d), lambda i, pos: (group_of_block(i, tm), 0, 0))],
            out_specs=[pl.BlockSpec((tm, d), lambda i, pos: (jnp.minimum(i, blocks_a - 1), 0)),
                       pl.BlockSpec((tm, d), lambda i, pos: (jnp.maximum(i - blocks_a, 0), 0))],
            scratch_shapes=[pltpu.VMEM((2, TOP_K * tm, d), F32), pltpu.SemaphoreType.DMA((2,))]),
        out_shape=[jax.ShapeDtypeStruct((rows_a, d), F32), jax.ShapeDtypeStruct((m - rows_a, d), F32)],
        compiler_params=_cparams(("arbitrary",)),
        name="moe_combine",
    )(pos_blocked, yb, gates, x1, g_post.reshape(1, d), gate2)


def _routing_tables(top_i, rank, counts, n_tok):
    m = n_tok * TOP_K
    n_sb = m // EXPERT_ROWS + N_EXPERTS
    n_rows = n_sb * EXPERT_ROWS
    flat_e = top_i.reshape(-1)
    rank = rank.reshape(-1)
    pcounts = (counts + EXPERT_ROWS - 1) // EXPERT_ROWS * EXPERT_ROWS
    pend = jnp.cumsum(pcounts)
    pstart = pend - pcounts
    pos = (pstart[flat_e] + rank).astype(jnp.int32)
    row_tok = jnp.zeros((n_rows,), jnp.int32).at[pos].set(jnp.arange(m, dtype=jnp.int32) // TOP_K)
    sb_row = jnp.arange(n_sb, dtype=jnp.int32) * EXPERT_ROWS
    sb_e = jnp.minimum(jnp.searchsorted(pend, sb_row, side='right'), N_EXPERTS - 1).astype(jnp.int32)
    sb_valid = jnp.clip(counts[sb_e] - (sb_row - pstart[sb_e]), 0, EXPERT_ROWS)
    sb_valid = jnp.where(sb_row < pend[-1], sb_valid, 0).astype(jnp.int32)
    last = jnp.maximum(pend[-1] // EXPERT_ROWS - 1, 0).astype(jnp.int32)
    sb_x = jnp.minimum(jnp.arange(n_sb, dtype=jnp.int32), last)
    sb_e = sb_e[sb_x]
    chunk_off = jnp.arange(EXPERT_ROWS // ROW_CHUNK, dtype=jnp.int32) * ROW_CHUNK
    chunk_valid = jnp.clip(sb_valid[:, None] - chunk_off[None, :], 0, ROW_CHUNK).reshape(-1).astype(jnp.int32)
    return pos, row_tok, chunk_valid, (sb_e, sb_valid, sb_x), n_rows


def _rot_cols(w):
    a1, a2, b1, b2 = jnp.split(w, 4, axis=-1)
    return jnp.concatenate([-a2, a1, -b2, b1], axis=-1)


def _rope_angles(n_tok):
    rows = n_tok // GRID_W
    row = jnp.repeat(jnp.arange(rows), GRID_W).astype(F32)
    col = jnp.tile(jnp.arange(GRID_W), rows).astype(F32)
    half = QK_ROPE // 2
    inv = ROPE_THETA ** (-jnp.arange(0, half, 2, dtype=F32) / half)
    ang_r = row[:, None] * inv
    ang_c = col[:, None] * inv
    ang = jnp.concatenate([ang_r, ang_r, ang_c, ang_c], axis=-1)
    return jnp.cos(ang), jnp.sin(ang)


def kernel(x_prompt, x_sample, cache_ckv, cache_krope, state_ssm_fwd, state_ssm_bwd, c, c_ctx, w_mod, b_mod, norm_mix_pre, norm_mix_post, norm_ffn_pre, norm_ffn_post, w_in, norm_q, norm_kv, w_uq, w_uk, w_uv, conv_w, conv_b, dt_bias_fwd, dt_bias_bwd, a_log_fwd, a_log_bwd, d_skip, norm_ssm, w_out, router_w, router_b, w_gate, b_gate, w_up, b_up, w_down, b_down):
    assert w_mod.shape[0] == 1, "single-layer trunk"
    n_ctx_seq, ctx_len, d = x_prompt.shape
    n_lat_seq, lat_len, _ = x_sample.shape
    past_len = cache_ckv.shape[2]
    n_ctx = n_ctx_seq * ctx_len
    n_lat = n_lat_seq * lat_len
    n_tok = n_ctx + n_lat
    n_groups = 1 + n_lat_seq

    def group_of_block(i, tm):
        row = i * tm
        return jnp.maximum(row - n_ctx, 0) // lat_len + (row >= n_ctx).astype(jnp.int32)

    cond = jnp.concatenate([c_ctx[None, :], c, jnp.zeros((SUBLANES - n_groups, d), F32)], axis=0)
    mod = _modulation(cond, w_mod[0], b_mod[0])[:n_groups]
    shift1, scale1, gate1, shift2, scale2, gate2 = [t.reshape(n_groups, 1, d) for t in jnp.split(mod, 6, axis=-1)]

    x_ctx = x_prompt.reshape(n_ctx, d)
    x_lat = x_sample.reshape(n_lat, d)
    h = _prenorm(x_ctx, x_lat, norm_mix_pre[0], scale1, shift1, group_of_block)

    o1, o2, o3, o4, o5 = Q_LORA, Q_LORA + KV_LORA, Q_LORA + KV_LORA + QK_ROPE, \
        Q_LORA + KV_LORA + QK_ROPE + SSM_WIDTH, Q_LORA + KV_LORA + QK_ROPE + SSM_WIDTH + CONV_DIM
    w_in0 = w_in[0]
    w_kr = w_in0[:, o2:o3]
    w_krdt = jnp.concatenate([w_kr, _rot_cols(w_kr), w_in0[:, o5:],
                              jnp.zeros((d, 2 * LANES - 2 * QK_ROPE - 2 * SSM_HEADS), F32)], axis=1)
    cq = _matmul(h, w_in0[:, :o1].astype(BF16), F32, 1024, Q_LORA, "in_proj_q")
    ckv = _matmul(h, w_in0[:, o1:o2].astype(BF16), F32, 1024, KV_LORA, "in_proj_kv")
    z = _matmul(h, w_in0[:, o3:o4].astype(BF16), F32, 1024, 512, "in_proj_z")
    xbc = _matmul(h, w_in0[:, o4:o5].astype(BF16), F32, 1024, 512, "in_proj_xbc")
    krdt = _matmul(h, w_krdt.astype(BF16), F32, 1024, 2 * LANES, "in_proj_krdt")

    cos, sin = _rope_angles(lat_len)
    ones = jnp.ones((n_ctx, QK_ROPE), F32)
    q_table = (QK_SCALE * math.log2(math.e)) * jnp.concatenate([
        jnp.concatenate([jnp.ones((n_ctx, QK_NOPE), F32), ones, 0.0 * ones], axis=1),
        jnp.tile(jnp.concatenate([jnp.ones((lat_len, QK_NOPE), F32), cos, sin], axis=1), (n_lat_seq, 1))], axis=0)
    k_table = jnp.concatenate([jnp.concatenate([ones, 0.0 * ones], axis=1),
                               jnp.tile(jnp.concatenate([cos, sin], axis=1), (n_lat_seq, 1))], axis=0)
    wq = w_uq[0].reshape(Q_LORA, MLA_HEADS, QK_NOPE + QK_ROPE)
    wq = jnp.concatenate([wq, _rot_cols(wq[..., QK_NOPE:])], axis=-1).reshape(Q_LORA, MLA_HEADS * QK_PAD)
    q = _q_proj(cq, norm_q[0], wq.astype(BF16), q_table)
    wk = w_uk[0].astype(BF16)
    wv = w_uv[0].astype(BF16)
    kk = _krope(krdt, k_table)
    ckr = cache_krope[:, 0].reshape(n_lat_seq * past_len, QK_ROPE)
    kk_c = jnp.concatenate([ckr, ckr], axis=1).astype(BF16)
    ckv_n, k, v = _kv_proj(ckv, norm_kv[0], wk, wv, kk, True, "kv_proj")
    _, k_c, v_c = _kv_proj(cache_ckv[:, 0].reshape(n_lat_seq * past_len, KV_LORA), norm_kv[0], wk, wv, kk_c,
                           False, "kv_proj_cache")

    attn_ctx = _attention(q, k, v, n_ctx_seq, ctx_len, ctx_len, 0, ctx_len, MLA_HEADS, "attn_ctx")

    def with_cache(lat, cached):
        lat = lat[n_ctx:].reshape(n_lat_seq, lat_len, -1)
        cached = cached.reshape(n_lat_seq, past_len, -1)
        return jnp.concatenate([lat, cached], axis=1).reshape(n_lat_seq * (lat_len + past_len), -1)

    attn_lat = _attention(q, with_cache(k, k_c), with_cache(v, v_c),
                          n_lat_seq, lat_len, lat_len + past_len, n_ctx, 2 * LANES, 1, "attn_lat")

    dt_bias = jnp.concatenate([dt_bias_fwd[0], dt_bias_bwd[0], jnp.zeros((LANES - 2 * SSM_HEADS,), F32)])
    a_log = jnp.concatenate([a_log_fwd[0], a_log_bwd[0], jnp.zeros((LANES - 2 * SSM_HEADS,), F32)])
    dt_arrays = _dt_prep(krdt, dt_bias.reshape(1, LANES), a_log.reshape(1, LANES))
    d_wide = jnp.repeat(d_skip[0], SSM_HEADDIM).reshape(1, SSM_WIDTH)
    zero_state = jnp.zeros((n_ctx_seq, SSM_WIDTH, D_STATE), F32)
    ssm = []
    fins = []
    for n_seq, seq_len, row0, tc, init_f, init_b in (
            (n_ctx_seq, ctx_len, 0, 4 * LANES, zero_state, zero_state),
            (n_lat_seq, lat_len, n_ctx, 2 * LANES,
             state_ssm_fwd[:, 0].reshape(n_lat_seq, SSM_WIDTH, D_STATE),
             state_ssm_bwd[:, 0].reshape(n_lat_seq, SSM_WIDTH, D_STATE))):
        tag = "ctx" if row0 == 0 else "lat"
        act = _conv_silu(xbc, conv_w[0], conv_b[0], n_seq, seq_len, row0, tc, "conv_" + tag)
        y_f, y_b, fin_f, fin_b = _ssd_scan(act, dt_arrays, init_f, init_b, n_seq, seq_len, row0, "ssd_" + tag)
        ssm.append(_ssd_out(y_f, y_b, act, z, d_wide, norm_ssm[0], row0))
        fins.append((fin_f, fin_b))

    attn = jnp.concatenate([attn_ctx, attn_lat], axis=0)
    ssm = jnp.concatenate(ssm, axis=0)
    w_o = w_out[0].astype(BF16)
    mix = _matmul2(attn, ssm, w_o[:MLA_WIDTH], w_o[MLA_WIDTH:], F32, 1024, 512, "out_proj")
    rw = jnp.concatenate([router_w[0], jnp.zeros((d, LANES - N_EXPERTS), F32)], axis=1)
    rb = jnp.concatenate([router_b[0], jnp.zeros((LANES - N_EXPERTS,), F32)]).reshape(1, LANES)
    x1, h2, top_i, gates, rank, counts = _postmix_router(
        x_ctx, x_lat, mix, norm_mix_post[0], gate1, norm_ffn_pre[0], scale2, shift2, rw, rb, group_of_block)

    pos, row_tok, chunk_valid, meta, n_rows = _routing_tables(
        top_i[:, :TOP_K], rank[:, :TOP_K], counts[0, :N_EXPERTS], n_tok)
    xg = _gather_rows(row_tok, chunk_valid, h2, n_rows)
    act = _expert_in(meta, xg, w_gate[0], w_up[0], b_gate[0].reshape(N_EXPERTS, 1, -1),
                     b_up[0].reshape(N_EXPERTS, 1, -1))
    yb = _expert_out(meta, act, w_down[0], b_down[0].reshape(N_EXPERTS, 1, -1))
    y_ctx, y_lat = _combine(pos, yb, gates, x1, norm_ffn_post[0], gate2, group_of_block, n_ctx)

    y_prompt = y_ctx.reshape(n_ctx_seq, ctx_len, d)
    y_sample = y_lat.reshape(n_lat_seq, lat_len, d)
    new_ckv = ckv_n[:n_ctx].reshape(n_ctx_seq, 1, ctx_len, KV_LORA)
    new_kr = krdt[:n_ctx, :QK_ROPE].reshape(n_ctx_seq, 1, ctx_len, QK_ROPE)
    fin_f, fin_b = fins[0]
    new_f = fin_f.reshape(n_ctx_seq, 1, SSM_HEADS, SSM_HEADDIM, D_STATE)
    new_b = fin_b.reshape(n_ctx_seq, 1, SSM_HEADS, SSM_HEADDIM, D_STATE)
    return (y_prompt, y_sample, new_ckv, new_kr, new_f, new_b)
```

```python
import functools
import math

import jax
import jax.numpy as jnp
from jax import lax
from jax.experimental import pallas as pl
from jax.experimental.pallas import tpu as pltpu

F32 = jnp.float32
BF16 = jnp.bfloat16
HIGHEST = lax.Precision.HIGHEST

D_MODEL = 4096
GRID_W = 64
NORM_EPS = 1e-6
ROPE_THETA = 10000.0
MLA_HEADS = 16
QK_NOPE = 128
QK_ROPE = 64
V_HEAD = 128
Q_LORA = 768
KV_LORA = 512
MLA_WIDTH = MLA_HEADS * V_HEAD
QK_SCALE = (QK_NOPE + QK_ROPE) ** -0.5
SSM_HEADDIM = 64
SSM_HEADS = 32
SSM_WIDTH = SSM_HEADS * SSM_HEADDIM
SSM_GROUPS = 4
HEADS_PER_GROUP = SSM_HEADS // SSM_GROUPS
GROUP_WIDTH = SSM_WIDTH // SSM_GROUPS
D_STATE = 128
D_CONV = 5
CHUNK = 128
CONV_DIM = SSM_WIDTH + 2 * SSM_GROUPS * D_STATE
N_EXPERTS = 32
TOP_K = 4
SWIGLU_LIMIT = 7.0
SWIGLU_ALPHA = 1.702

LANES = 128
SUBLANES = 8
BF16_ROWS = 16
V7X_VMEM_LIMIT_BYTES = 56 * 1024 * 1024

QK_PAD = 2 * LANES
EXPERT_ROWS = 2304
ROW_CHUNK = 256
COMBINE_TOKENS = 64
DMA_UNROLL = 8


def _cparams(sem, vmem=V7X_VMEM_LIMIT_BYTES):
    return pltpu.CompilerParams(dimension_semantics=sem, vmem_limit_bytes=vmem)


def _sigmoid(x):
    return 1.0 / (1.0 + jnp.exp(-x))


def _rms_rows(x):
    return x * lax.rsqrt(jnp.mean(x * x, axis=-1, keepdims=True) + NORM_EPS)


def _mod_kernel(c_ref, w_ref, b_ref, o_ref):
    c = c_ref[...]
    s = (c * _sigmoid(c)).astype(BF16)
    o_ref[...] = jnp.dot(s, w_ref[...].astype(BF16), preferred_element_type=F32) + b_ref[...]


def _modulation(cond, w_mod, b_mod, tn=512):
    rows, k = cond.shape
    n = w_mod.shape[1]
    return pl.pallas_call(
        _mod_kernel,
        grid=(n // tn,),
        in_specs=[pl.BlockSpec((rows, k), lambda j: (0, 0)),
                  pl.BlockSpec((k, tn), lambda j: (0, j)),
                  pl.BlockSpec((1, tn), lambda j: (0, j))],
        out_specs=pl.BlockSpec((rows, tn), lambda j: (0, j)),
        out_shape=jax.ShapeDtypeStruct((rows, n), F32),
        compiler_params=_cparams(("parallel",)),
        name="modulation",
    )(cond, w_mod, b_mod.reshape(1, n))


def _prenorm_kernel(xa_ref, xb_ref, g_ref, sc_ref, sh_ref, o_ref, *, blocks_a):
    x = jnp.where(pl.program_id(0) < blocks_a, xa_ref[...], xb_ref[...])
    h = _rms_rows(x) * g_ref[...]
    o_ref[...] = (h * (1.0 + sc_ref[0]) + sh_ref[0]).astype(o_ref.dtype)


def _prenorm(xa, xb, gain, scale, shift, group_of_block, tm=256):
    d = xa.shape[1]
    m = xa.shape[0] + xb.shape[0]
    blocks_a = xa.shape[0] // tm
    return pl.pallas_call(
        functools.partial(_prenorm_kernel, blocks_a=blocks_a),
        grid=(m // tm,),
        in_specs=[pl.BlockSpec((tm, d), lambda i: (jnp.minimum(i, blocks_a - 1), 0)),
                  pl.BlockSpec((tm, d), lambda i: (jnp.maximum(i - blocks_a, 0), 0)),
                  pl.BlockSpec((1, d), lambda i: (0, 0)),
                  pl.BlockSpec((1, 1, d), lambda i: (group_of_block(i, tm), 0, 0)),
                  pl.BlockSpec((1, 1, d), lambda i: (group_of_block(i, tm), 0, 0))],
        out_specs=pl.BlockSpec((tm, d), lambda i: (i, 0)),
        out_shape=jax.ShapeDtypeStruct((m, d), BF16),
        compiler_params=_cparams(("parallel",)),
        name="prenorm",
    )(xa, xb, gain.reshape(1, d), scale, shift)


def _column_blocks(w, tn):
    k, n = w.shape
    return w.astype(BF16).reshape(k, n // tn, tn).transpose(1, 0, 2)


def _mm_kernel(a_ref, b_ref, o_ref):
    for r in range(a_ref.shape[0] // ROW_CHUNK):
        rows = slice(r * ROW_CHUNK, (r + 1) * ROW_CHUNK)
        o_ref[rows, :] = jnp.dot(a_ref[rows, :], b_ref[0], preferred_element_type=F32).astype(o_ref.dtype)


def _matmul(a, b_blocks, out_dtype, tm, name):
    m, k = a.shape
    n_blocks, _, tn = b_blocks.shape
    return pl.pallas_call(
        _mm_kernel,
        grid=(m // tm, n_blocks),
        in_specs=[pl.BlockSpec((tm, k), lambda i, j: (i, 0)),
                  pl.BlockSpec((1, k, tn), lambda i, j: (j, 0, 0))],
        out_specs=pl.BlockSpec((tm, tn), lambda i, j: (i, j)),
        out_shape=jax.ShapeDtypeStruct((m, n_blocks * tn), out_dtype),
        compiler_params=_cparams(("parallel", "parallel")),
        name=name,
    )(a, b_blocks)


def _mm2_kernel(a1_ref, a2_ref, b1_ref, b2_ref, o_ref):
    for r in range(a1_ref.shape[0] // ROW_CHUNK):
        rows = slice(r * ROW_CHUNK, (r + 1) * ROW_CHUNK)
        acc = jnp.dot(a1_ref[rows, :], b1_ref[0], preferred_element_type=F32)
        acc = acc + jnp.dot(a2_ref[rows, :], b2_ref[0], preferred_element_type=F32)
        o_ref[rows, :] = acc.astype(o_ref.dtype)


def _matmul2(a1, a2, b1_blocks, b2_blocks, out_dtype, tm, name):
    m, k1 = a1.shape
    k2 = a2.shape[1]
    n_blocks, _, tn = b1_blocks.shape
    return pl.pallas_call(
        _mm2_kernel,
        grid=(m // tm, n_blocks),
        in_specs=[pl.BlockSpec((tm, k1), lambda i, j: (i, 0)),
                  pl.BlockSpec((tm, k2), lambda i, j: (i, 0)),
                  pl.BlockSpec((1, k1, tn), lambda i, j: (j, 0, 0)),
                  pl.BlockSpec((1, k2, tn), lambda i, j: (j, 0, 0))],
        out_specs=pl.BlockSpec((tm, tn), lambda i, j: (i, j)),
        out_shape=jax.ShapeDtypeStruct((m, n_blocks * tn), out_dtype),
        compiler_params=_cparams(("parallel", "parallel")),
        name=name,
    )(a1, a2, b1_blocks, b2_blocks)


def _q_kernel(cq_ref, g_ref, w_ref, tab_ref, o_ref):
    xn = (_rms_rows(cq_ref[...]) * g_ref[...]).astype(BF16)
    tab = tab_ref[...]
    for h in range(MLA_HEADS):
        sl = slice(h * QK_PAD, (h + 1) * QK_PAD)
        q = jnp.dot(xn, w_ref[:, sl], preferred_element_type=F32)
        o_ref[:, sl] = (q * tab).astype(o_ref.dtype)


def _q_proj(cq, norm_q, w_q, q_table, tm=512):
    m, k = cq.shape
    n = w_q.shape[1]
    return pl.pallas_call(
        _q_kernel,
        grid=(m // tm,),
        in_specs=[pl.BlockSpec((tm, k), lambda i: (i, 0)),
                  pl.BlockSpec((1, k), lambda i: (0, 0)),
                  pl.BlockSpec((k, n), lambda i: (0, 0)),
                  pl.BlockSpec((tm, QK_PAD), lambda i: (i, 0))],
        out_specs=pl.BlockSpec((tm, n), lambda i: (i, 0)),
        out_shape=jax.ShapeDtypeStruct((m, n), BF16),
        compiler_params=_cparams(("parallel",)),
        name="q_proj",
    )(cq, norm_q.reshape(1, k), w_q, q_table)


def _kv_kernel(ckv_ref, g_ref, wk_ref, wv_ref, kk_ref, n_ref, k_ref, v_ref, *, normalise):
    x = ckv_ref[...]
    if normalise:
        x = _rms_rows(x) * g_ref[...]
    n_ref[...] = x
    xb = x.astype(BF16)
    kk = kk_ref[...]
    for h in range(MLA_HEADS):
        ws = slice(h * QK_NOPE, (h + 1) * QK_NOPE)
        k_ref[:, h * QK_PAD:h * QK_PAD + QK_NOPE] = jnp.dot(
            xb, wk_ref[:, ws], preferred_element_type=F32).astype(k_ref.dtype)
        k_ref[:, h * QK_PAD + QK_NOPE:(h + 1) * QK_PAD] = kk
    v_ref[...] = jnp.dot(xb, wv_ref[...], preferred_element_type=F32).astype(v_ref.dtype)


def _kv_proj(ckv, norm_kv, w_k, w_v, kk, normalise, name, tm=512):
    m, k = ckv.shape
    n = w_k.shape[1]
    return pl.pallas_call(
        functools.partial(_kv_kernel, normalise=normalise),
        grid=(m // tm,),
        in_specs=[pl.BlockSpec((tm, k), lambda i: (i, 0)),
                  pl.BlockSpec((1, k), lambda i: (0, 0)),
                  pl.BlockSpec((k, n), lambda i: (0, 0)),
                  pl.BlockSpec((k, n), lambda i: (0, 0)),
                  pl.BlockSpec((tm, LANES), lambda i: (i, 0))],
        out_specs=[pl.BlockSpec((tm, k), lambda i: (i, 0)),
                   pl.BlockSpec((tm, MLA_HEADS * QK_PAD), lambda i: (i, 0)),
                   pl.BlockSpec((tm, n), lambda i: (i, 0))],
        out_shape=[jax.ShapeDtypeStruct((m, k), F32),
                   jax.ShapeDtypeStruct((m, MLA_HEADS * QK_PAD), BF16),
                   jax.ShapeDtypeStruct((m, n), BF16)],
        compiler_params=_cparams(("parallel",)),
        name=name,
    )(ckv, norm_kv.reshape(1, k), w_k, w_v, kk)


def _krope_kernel(kr_ref, tab_ref, o_ref):
    t = kr_ref[...] * tab_ref[...]
    o_ref[...] = (t + pltpu.roll(t, QK_ROPE, 1)).astype(o_ref.dtype)


def _krope(krdt, k_table, tm=1024):
    m = krdt.shape[0]
    return pl.pallas_call(
        _krope_kernel,
        grid=(m // tm,),
        in_specs=[pl.BlockSpec((tm, LANES), lambda i: (i, 0)),
                  pl.BlockSpec((tm, LANES), lambda i: (i, 0))],
        out_specs=pl.BlockSpec((tm, LANES), lambda i: (i, 0)),
        out_shape=jax.ShapeDtypeStruct((m, LANES), BF16),
        compiler_params=_cparams(("parallel",)),
        name="krope",
    )(krdt, k_table)


def _attn_kernel(q_ref, k_ref, v_ref, o_ref, *, heads):
    for h in range(heads):
        q = q_ref[:, h * QK_PAD:(h + 1) * QK_PAD]
        k = k_ref[:, h * QK_PAD:(h + 1) * QK_PAD]
        s = lax.dot_general(q, k, (((1,), (1,)), ((), ())), preferred_element_type=F32)
        p = jnp.exp2(s - jnp.max(s, axis=1, keepdims=True))
        denom = jnp.sum(p, axis=1, keepdims=True)
        o = jnp.dot(p.astype(BF16), v_ref[:, h * V_HEAD:(h + 1) * V_HEAD], preferred_element_type=F32)
        o_ref[:, h * V_HEAD:(h + 1) * V_HEAD] = (o / denom).astype(o_ref.dtype)


def _attention(q, k, v, n_batch, s_len, kv_len, q_row0, tq, heads, name):
    nq = s_len // tq
    q_blk0 = q_row0 // tq
    return pl.pallas_call(
        functools.partial(_attn_kernel, heads=heads),
        grid=(n_batch, MLA_HEADS // heads, nq),
        in_specs=[pl.BlockSpec((tq, heads * QK_PAD), lambda b, h, qi: (q_blk0 + b * nq + qi, h)),
                  pl.BlockSpec((kv_len, heads * QK_PAD), lambda b, h, qi: (b, h)),
                  pl.BlockSpec((kv_len, heads * V_HEAD), lambda b, h, qi: (b, h))],
        out_specs=pl.BlockSpec((tq, heads * V_HEAD), lambda b, h, qi: (b * nq + qi, h)),
        out_shape=jax.ShapeDtypeStruct((n_batch * s_len, MLA_WIDTH), BF16),
        compiler_params=_cparams(("parallel", "parallel", "parallel")),
        name=name,
    )(q, k, v)


def _conv_kernel(u_ref, w_ref, b_ref, o_ref, pad_ref, *, seq_len):
    halo = SUBLANES
    width = u_ref.shape[-1]
    pad_ref[0:halo, :] = jnp.zeros((halo, width), F32)
    pad_ref[halo + seq_len:2 * halo + seq_len, :] = jnp.zeros((halo, width), F32)
    pad_ref[halo:halo + seq_len, :] = u_ref[...]
    for r in range(seq_len // ROW_CHUNK):
        acc = b_ref[...]
        for k in range(D_CONV):
            start = halo - D_CONV // 2 + k + r * ROW_CHUNK
            acc = acc + w_ref[k:k + 1, :] * pad_ref[start:start + ROW_CHUNK, :]
        o_ref[r * ROW_CHUNK:(r + 1) * ROW_CHUNK, :] = acc * _sigmoid(acc)


def _conv_silu(xbc, conv_w, conv_b, n_seq, seq_len, row0, tc, name):
    width = xbc.shape[1]
    blk0 = row0 // seq_len
    return pl.pallas_call(
        functools.partial(_conv_kernel, seq_len=seq_len),
        grid=(n_seq, width // tc),
        in_specs=[pl.BlockSpec((seq_len, tc), lambda s, c: (blk0 + s, c)),
                  pl.BlockSpec((D_CONV, tc), lambda s, c: (0, c)),
                  pl.BlockSpec((1, tc), lambda s, c: (0, c))],
        out_specs=pl.BlockSpec((seq_len, tc), lambda s, c: (s, c)),
        out_shape=jax.ShapeDtypeStruct((n_seq * seq_len, width), F32),
        scratch_shapes=[pltpu.VMEM((seq_len + 2 * SUBLANES, tc), F32)],
        compiler_params=_cparams(("parallel", "parallel")),
        name=name,
    )(xbc, conv_w, conv_b.reshape(1, width))


def _dt_kernel(raw_ref, bias_ref, alog_ref, dtf_ref, csf_ref, cstf_ref, dtb_ref, csb_ref, cstb_ref):
    raw = raw_ref[...] + bias_ref[...]
    dt = jnp.maximum(raw, 0.0) + jnp.log1p(jnp.exp(-jnp.abs(raw)))
    a = dt * (-jnp.exp(alog_ref[...]))
    row = lax.broadcasted_iota(jnp.int32, (CHUNK, CHUNK), 0)
    col = lax.broadcasted_iota(jnp.int32, (CHUNK, CHUNK), 1)
    lower = (row >= col).astype(F32)
    upper = (row <= col).astype(F32)
    prefix = jnp.dot(lower, a, precision=HIGHEST, preferred_element_type=F32)
    suffix = jnp.dot(upper, a, precision=HIGHEST, preferred_element_type=F32)
    dtf_ref[...] = dt
    csf_ref[...] = prefix
    cstf_ref[...] = prefix.T
    suffix_b = pltpu.roll(suffix, LANES - SSM_HEADS, 1)
    dtb_ref[...] = pltpu.roll(dt, LANES - SSM_HEADS, 1)
    csb_ref[...] = suffix_b
    cstb_ref[...] = suffix_b.T


def _dt_prep(krdt, dt_bias, a_log):
    m = krdt.shape[0]
    shp = jax.ShapeDtypeStruct((m, LANES), F32)
    spec = pl.BlockSpec((CHUNK, LANES), lambda i: (i, 0))
    return pl.pallas_call(
        _dt_kernel,
        grid=(m // CHUNK,),
        in_specs=[pl.BlockSpec((CHUNK, LANES), lambda i: (i, 1)),
                  pl.BlockSpec((1, LANES), lambda i: (0, 0)),
                  pl.BlockSpec((1, LANES), lambda i: (0, 0))],
        out_specs=[spec] * 6,
        out_shape=[shp] * 6,
        compiler_params=_cparams(("parallel",)),
        name="dt_prep",
    )(krdt, dt_bias, a_log)


def _replicate_lanes(x, e_ref):
    hi = x.astype(BF16)
    rest = x - hi.astype(F32)
    mid = rest.astype(BF16)
    lo = (rest - mid.astype(F32)).astype(BF16)
    e = e_ref[...]
    out = jnp.dot(hi, e, preferred_element_type=F32)
    out = out + jnp.dot(mid, e, preferred_element_type=F32)
    return out + jnp.dot(lo, e, preferred_element_type=F32)


def _ssd_direction(x_ref, b_ref, c_ref, dt_ref, cs_ref, cst_ref, y_ref, st_ref, e_head_ref, e_lane_ref,
                   reverse):
    ii = lax.broadcasted_iota(jnp.int32, (CHUNK, CHUNK), 0)
    jj = lax.broadcasted_iota(jnp.int32, (CHUNK, CHUNK), 1)
    mask = (jj >= ii) if reverse else (ii >= jj)
    cs = cs_ref[...]
    cst = cst_ref[...]
    edge = cs[0:BF16_ROWS, :] if reverse else cs[CHUNK - BF16_ROWS:CHUNK, :]
    wide = _replicate_lanes(jnp.concatenate([dt_ref[...], cs, edge], axis=0), e_head_ref)
    dt_w = wide[0:CHUNK]
    p_w = wide[CHUNK:2 * CHUNK]
    edge_row = 2 * CHUNK + (0 if reverse else BF16_ROWS - 1)
    tot_w = wide[edge_row:edge_row + 1]
    p_rep = _replicate_lanes(cs, e_lane_ref)
    xdt = x_ref[...] * dt_w
    xdec = (xdt * jnp.exp(tot_w - p_w)).astype(BF16)
    scale_in = jnp.exp(p_w)
    decay_tot = jnp.exp(tot_w)
    lane_w = lax.broadcasted_iota(jnp.int32, xdt.shape, 1)
    first_of_pair = (lane_w & SSM_HEADDIM) == 0
    x_first = jnp.where(first_of_pair, xdt, 0.0).astype(BF16)
    x_second = jnp.where(first_of_pair, 0.0, xdt).astype(BF16)
    for g in range(SSM_GROUPS):
        gs = slice(g * D_STATE, (g + 1) * D_STATE)
        gw = slice(g * GROUP_WIDTH, (g + 1) * GROUP_WIDTH)
        b_f32 = b_ref[:, gs]
        bg = b_f32.astype(BF16)
        bgt = b_f32.T.astype(BF16)
        cg = c_ref[:, gs].astype(BF16)
        cb = lax.dot_general(cg, bg, (((1,), (1,)), ((), ())), preferred_element_type=F32)
        st_old = st_ref[:, gw]
        y_off = jnp.dot(cg, st_old.astype(BF16), preferred_element_type=F32)
        st_ref[:, gw] = st_old * decay_tot[:, gw] + jnp.dot(bgt, xdec[:, gw], preferred_element_type=F32)
        for pair in range(HEADS_PER_GROUP // 2):
            h0 = g * HEADS_PER_GROUP + 2 * pair
            ps = slice(h0 * SSM_HEADDIM, (h0 + 2) * SSM_HEADDIM)
            acc = None
            for h, xm in ((h0, x_first), (h0 + 1, x_second)):
                w = jnp.where(mask, jnp.exp(p_rep[:, h * LANES:(h + 1) * LANES] - cst[h:h + 1, :]), 0.0)
                term = jnp.dot((cb * w).astype(BF16), xm[:, ps], preferred_element_type=F32)
                acc = term if acc is None else acc + term
            y_ref[:, ps] = acc + y_off[:, 2 * pair * SSM_HEADDIM:(2 * pair + 2) * SSM_HEADDIM] * scale_in[:, ps]


def _ssd_kernel(xf_ref, bf_ref, cf_ref, dtf_ref, csf_ref, cstf_ref, initf_ref,
                xb_ref, bb_ref, cb_ref, dtb_ref, csb_ref, cstb_ref, initb_ref,
                yf_ref, yb_ref, finf_ref, finb_ref, stf_ref, stb_ref, e_head_ref, e_lane_ref, *, n_chunks):
    c = pl.program_id(1)

    @pl.when(c == 0)
    def _():
        stf_ref[...] = initf_ref[0].T
        stb_ref[...] = initb_ref[0].T
        row = lax.broadcasted_iota(jnp.int32, e_head_ref.shape, 0)
        col = lax.broadcasted_iota(jnp.int32, e_head_ref.shape, 1)
        e_head_ref[...] = jnp.where((col >> (SSM_HEADDIM.bit_length() - 1)) == row, 1.0, 0.0).astype(BF16)
        row = lax.broadcasted_iota(jnp.int32, e_lane_ref.shape, 0)
        col = lax.broadcasted_iota(jnp.int32, e_lane_ref.shape, 1)
        e_lane_ref[...] = jnp.where((col >> (LANES.bit_length() - 1)) == row, 1.0, 0.0).astype(BF16)

    _ssd_direction(xf_ref, bf_ref, cf_ref, dtf_ref, csf_ref, cstf_ref, yf_ref, stf_ref,
                   e_head_ref, e_lane_ref, False)
    _ssd_direction(xb_ref, bb_ref, cb_ref, dtb_ref, csb_ref, cstb_ref, yb_ref, stb_ref,
                   e_head_ref, e_lane_ref, True)

    @pl.when(c == n_chunks - 1)
    def _():
        finf_ref[0] = stf_ref[...].T
        finb_ref[0] = stb_ref[...].T


def _ssd_scan(xbc, dt_arrays, init_f, init_b, n_seq, seq_len, row0, name):
    n_chunks = seq_len // CHUNK
    blk0 = row0 // CHUNK
    xblk = SSM_WIDTH // (SSM_GROUPS * D_STATE)
    dt_f, cs_f, cst_f, dt_b, cs_b, cst_b = dt_arrays

    def direction_specs(chunk):
        small = pl.BlockSpec((CHUNK, LANES), lambda s, c: (blk0 + chunk(s, c), 0))
        return [pl.BlockSpec((CHUNK, SSM_WIDTH), lambda s, c: (chunk(s, c), 0)),
                pl.BlockSpec((CHUNK, SSM_GROUPS * D_STATE), lambda s, c: (chunk(s, c), xblk)),
                pl.BlockSpec((CHUNK, SSM_GROUPS * D_STATE), lambda s, c: (chunk(s, c), xblk + 1)),
                small, small, small,
                pl.BlockSpec((1, SSM_WIDTH, D_STATE), lambda s, c: (s, 0, 0))]

    fwd = lambda s, c: s * n_chunks + c
    bwd = lambda s, c: s * n_chunks + (n_chunks - 1 - c)
    state_spec = pl.BlockSpec((1, SSM_WIDTH, D_STATE), lambda s, c: (s, 0, 0))
    y_shape = jax.ShapeDtypeStruct((n_seq * seq_len, SSM_WIDTH), F32)
    state_shape = jax.ShapeDtypeStruct((n_seq, SSM_WIDTH, D_STATE), F32)
    return pl.pallas_call(
        functools.partial(_ssd_kernel, n_chunks=n_chunks),
        grid=(n_seq, n_chunks),
        in_specs=direction_specs(fwd) + direction_specs(bwd),
        out_specs=[pl.BlockSpec((CHUNK, SSM_WIDTH), lambda s, c: (fwd(s, c), 0)),
                   pl.BlockSpec((CHUNK, SSM_WIDTH), lambda s, c: (bwd(s, c), 0)),
                   state_spec, state_spec],
        out_shape=[y_shape, y_shape, state_shape, state_shape],
        scratch_shapes=[pltpu.VMEM((D_STATE, SSM_WIDTH), F32), pltpu.VMEM((D_STATE, SSM_WIDTH), F32),
                        pltpu.VMEM((LANES, SSM_WIDTH), BF16), pltpu.VMEM((LANES, SSM_HEADS * LANES), BF16)],
        compiler_params=_cparams(("parallel", "arbitrary")),
        name=name,
    )(xbc, xbc, xbc, dt_f, cs_f, cst_f, init_f, xbc, xbc, xbc, dt_b, cs_b, cst_b, init_b)


def _ssd_out_kernel(yf_ref, yb_ref, x_ref, z_ref, d_ref, g_ref, o_ref):
    for g in range(SSM_GROUPS):
        gs = slice(g * GROUP_WIDTH, (g + 1) * GROUP_WIDTH)
        z = z_ref[:, gs]
        y = (yf_ref[:, gs] + yb_ref[:, gs] + d_ref[:, gs] * x_ref[:, gs]) * (z * _sigmoid(z))
        o_ref[:, gs] = (_rms_rows(y) * g_ref[:, gs]).astype(o_ref.dtype)


def _ssd_out(y_f, y_b, xbc, z, d_wide, norm_ssm, row0, tm=256):
    m = y_f.shape[0]
    blk0 = row0 // tm
    return pl.pallas_call(
        _ssd_out_kernel,
        grid=(m // tm,),
        in_specs=[pl.BlockSpec((tm, SSM_WIDTH), lambda i: (i, 0)),
                  pl.BlockSpec((tm, SSM_WIDTH), lambda i: (i, 0)),
                  pl.BlockSpec((tm, SSM_WIDTH), lambda i: (i, 0)),
                  pl.BlockSpec((tm, SSM_WIDTH), lambda i: (blk0 + i, 0)),
                  pl.BlockSpec((1, SSM_WIDTH), lambda i: (0, 0)),
                  pl.BlockSpec((1, SSM_WIDTH), lambda i: (0, 0))],
        out_specs=pl.BlockSpec((tm, SSM_WIDTH), lambda i: (i, 0)),
        out_shape=jax.ShapeDtypeStruct((m, SSM_WIDTH), BF16),
        compiler_params=_cparams(("parallel",)),
        name="ssd_out",
    )(y_f, y_b, xbc, z, d_wide, norm_ssm.reshape(1, SSM_WIDTH))


def _postmix_kernel(xa_ref, xb_ref, mix_ref, gpost_ref, gate1_ref, gpre_ref, sc_ref, sh_ref, rw_ref, rb_ref,
                    x1_ref, h_ref, idx_ref, gates_ref, rank_ref, counts_ref, run_ref, *, blocks_a):
    i = pl.program_id(0)

    @pl.when(i == 0)
    def _():
        run_ref[...] = jnp.zeros(run_ref.shape, F32)

    x = jnp.where(i < blocks_a, xa_ref[...], xb_ref[...])
    x1 = x + gate1_ref[0] * (_rms_rows(mix_ref[...]) * gpost_ref[...])
    x1_ref[...] = x1
    h = (_rms_rows(x1) * gpre_ref[...]) * (1.0 + sc_ref[0]) + sh_ref[0]
    h_ref[...] = h
    logits = jnp.dot(h, rw_ref[...], precision=HIGHEST, preferred_element_type=F32) + rb_ref[...]
    lane = lax.broadcasted_iota(jnp.int32, logits.shape, 1)
    lane_f = lane.astype(F32)
    live = jnp.where(lane < N_EXPERTS, logits, -jnp.inf)
    vals, idxs = [], []
    for _ in range(TOP_K):
        top = jnp.max(live, axis=1, keepdims=True)
        first = jnp.min(jnp.where(live == top, lane_f, float(LANES)), axis=1, keepdims=True)
        vals.append(top)
        idxs.append(first)
        live = jnp.where(lane_f == first, -jnp.inf, live)
    exps = [jnp.exp(v - vals[0]) for v in vals]
    denom = exps[0]
    for e in exps[1:]:
        denom = denom + e
    gate_out = jnp.zeros(logits.shape, F32)
    idx_out = jnp.zeros(logits.shape, F32)
    for k in range(TOP_K):
        gate_out = jnp.where(lane == k, exps[k] / denom, gate_out)
        idx_out = jnp.where(lane == k, idxs[k], idx_out)
    gates_ref[...] = gate_out
    idx_ref[...] = idx_out.astype(jnp.int32)

    hots = [jnp.where(lane_f == idxs[k], 1.0, 0.0) for k in range(TOP_K)]
    multi = hots[0]
    for hot in hots[1:]:
        multi = multi + hot
    n_rows = multi.shape[0]
    earlier = (lax.broadcasted_iota(jnp.int32, (n_rows, n_rows), 0)
               > lax.broadcasted_iota(jnp.int32, (n_rows, n_rows), 1))
    before = run_ref[0:1, :] + jnp.dot(jnp.where(earlier, 1.0, 0.0).astype(BF16), multi.astype(BF16),
                                       preferred_element_type=F32)
    rank_out = jnp.zeros(logits.shape, F32)
    for k in range(TOP_K):
        rank_out = jnp.where(lane == k, jnp.sum(hots[k] * before, axis=1, keepdims=True), rank_out)
    rank_ref[...] = rank_out.astype(jnp.int32)
    run_ref[...] = run_ref[...] + jnp.sum(multi, axis=0, keepdims=True)
    counts_ref[...] = run_ref[...].astype(jnp.int32)


def _postmix_router(xa, xb, mix, g_post, gate1, g_pre, scale2, shift2, router_w, router_b,
                    group_of_block, tm=128):
    d = xa.shape[1]
    m = xa.shape[0] + xb.shape[0]
    blocks_a = xa.shape[0] // tm
    row = lambda i: (i, 0)
    fixed = lambda i: (0, 0)
    grp = lambda i: (group_of_block(i, tm), 0, 0)
    wide = jax.ShapeDtypeStruct((m, d), F32)
    narrow_i = jax.ShapeDtypeStruct((m, LANES), jnp.int32)
    return pl.pallas_call(
        functools.partial(_postmix_kernel, blocks_a=blocks_a),
        grid=(m // tm,),
        in_specs=[pl.BlockSpec((tm, d), lambda i: (jnp.minimum(i, blocks_a - 1), 0)),
                  pl.BlockSpec((tm, d), lambda i: (jnp.maximum(i - blocks_a, 0), 0)),
                  pl.BlockSpec((tm, d), row),
                  pl.BlockSpec((1, d), fixed), pl.BlockSpec((1, 1, d), grp),
                  pl.BlockSpec((1, d), fixed), pl.BlockSpec((1, 1, d), grp), pl.BlockSpec((1, 1, d), grp),
                  pl.BlockSpec((d, LANES), fixed), pl.BlockSpec((1, LANES), fixed)],
        out_specs=[pl.BlockSpec((tm, d), row), pl.BlockSpec((tm, d), row),
                   pl.BlockSpec((tm, LANES), row), pl.BlockSpec((tm, LANES), row),
                   pl.BlockSpec((tm, LANES), row), pl.BlockSpec((SUBLANES, LANES), fixed)],
        out_shape=[wide, wide, narrow_i, jax.ShapeDtypeStruct((m, LANES), F32), narrow_i,
                   jax.ShapeDtypeStruct((SUBLANES, LANES), jnp.int32)],
        scratch_shapes=[pltpu.VMEM((SUBLANES, LANES), F32)],
        compiler_params=_cparams(("arbitrary",)),
        name="postmix_router",
    )(xa, xb, mix, g_post.reshape(1, d), gate1, g_pre.reshape(1, d), scale2, shift2, router_w, router_b)


def _row_copy(src_hbm, dst_vmem, sem, src_row, dst_row):
    return pltpu.make_async_copy(src_hbm.at[pl.ds(src_row, 1)], dst_vmem.at[pl.ds(dst_row, 1)], sem)


def _gather_kernel(tok_ref, nv_ref, src_ref, nlive_ref, h_hbm, o_ref, buf_ref, sem):
    i = pl.program_id(0)

    def fetch(blk, slot):
        n_live = nv_ref[blk]
        base = src_ref[blk]

        def issue(r, carry):
            _row_copy(h_hbm, buf_ref.at[slot], sem.at[slot], tok_ref[base + r], r).start()
            return carry

        @pl.when(n_live == ROW_CHUNK)
        def _():
            lax.fori_loop(0, ROW_CHUNK, issue, 0, unroll=DMA_UNROLL)

        @pl.when((n_live > 0) & (n_live < ROW_CHUNK))
        def _():
            buf_ref[slot] = jnp.zeros(buf_ref.shape[1:], F32)
            lax.fori_loop(0, n_live, issue, 0)

    @pl.when(i == 0)
    def _():
        fetch(0, 0)

    slot = lax.rem(i, 2)

    @pl.when(i + 1 < pl.num_programs(0))
    def _():
        fetch(i + 1, 1 - slot)

    nv = nv_ref[i]

    @pl.when(nv == ROW_CHUNK)
    def _():
        pltpu.make_async_copy(h_hbm.at[pl.ds(0, ROW_CHUNK)], buf_ref.at[slot], sem.at[slot]).wait()

    @pl.when((nv > 0) & (nv < ROW_CHUNK))
    def _():
        def drain(r, carry):
            _row_copy(h_hbm, buf_ref.at[slot], sem.at[slot], 0, r).wait()
            return carry

        lax.fori_loop(0, nv, drain, 0)

    @pl.when(nv > 0)
    def _():
        o_ref[...] = buf_ref[slot].astype(o_ref.dtype)

    @pl.when(nv == 0)
    def _():
        o_ref[...] = jnp.zeros(o_ref.shape, o_ref.dtype)


def _gather_rows(row_tok, chunk_valid, chunk_src, n_live_chunks, h, n_rows):
    d = h.shape[1]
    n_chunks = n_rows // ROW_CHUNK - EXPERT_ROWS // ROW_CHUNK
    return pl.pallas_call(
        _gather_kernel,
        grid_spec=pltpu.PrefetchScalarGridSpec(
            num_scalar_prefetch=4,
            grid=(n_chunks,),
            in_specs=[pl.BlockSpec(memory_space=pl.ANY)],
            out_specs=pl.BlockSpec((ROW_CHUNK, d),
                                   lambda i, tok, nv, src, nlive: (jnp.where(i < nlive[0], i, n_chunks), 0)),
            scratch_shapes=[pltpu.VMEM((2, ROW_CHUNK, d), F32), pltpu.SemaphoreType.DMA((2,))]),
        out_shape=jax.ShapeDtypeStruct((n_rows, d), BF16),
        compiler_params=_cparams(("arbitrary",)),
        name="moe_gather",
    )(row_tok, chunk_valid, chunk_src, n_live_chunks, h)


def _over_live_rows(nv, compute, o_ref):
    n_chunks = (nv + ROW_CHUNK - 1) // ROW_CHUNK
    n_pairs = n_chunks // 2

    def pair(p, carry):
        compute(pl.multiple_of(p * (2 * ROW_CHUNK), 2 * ROW_CHUNK), 2 * ROW_CHUNK)
        return carry

    lax.fori_loop(0, n_pairs, pair, 0)

    @pl.when(n_chunks % 2 == 1)
    def _():
        compute(pl.multiple_of(n_pairs * (2 * ROW_CHUNK), ROW_CHUNK), ROW_CHUNK)

    def zero(c, carry):
        o_ref[pl.ds(pl.multiple_of(c * ROW_CHUNK, ROW_CHUNK), ROW_CHUNK), :] = jnp.zeros(
            (ROW_CHUNK, o_ref.shape[1]), o_ref.dtype)
        return carry

    lax.fori_loop(n_chunks, EXPERT_ROWS // ROW_CHUNK, zero, 0)


def _expert_in_kernel(sbe_ref, sbv_ref, sbx_ref, x_ref, wg_ref, wu_ref, bg_ref, bu_ref, o_ref,
                      wgb_ref, wub_ref):
    nv = sbv_ref[pl.program_id(0)]

    @pl.when(nv > 0)
    def _():
        wgb_ref[...] = wg_ref[0].astype(BF16)
        wub_ref[...] = wu_ref[0].astype(BF16)

    def compute(r0, n):
        rows = pl.ds(r0, n)
        xs = x_ref[rows, :]
        g = jnp.dot(xs, wgb_ref[...], preferred_element_type=F32) + bg_ref[0]
        u = jnp.dot(xs, wub_ref[...], preferred_element_type=F32) + bu_ref[0]
        g = jnp.minimum(g, SWIGLU_LIMIT)
        u = jnp.clip(u, -SWIGLU_LIMIT, SWIGLU_LIMIT)
        o_ref[rows, :] = (g * _sigmoid(SWIGLU_ALPHA * g) * (u + 1.0)).astype(o_ref.dtype)

    _over_live_rows(nv, compute, o_ref)


def _expert_out_kernel(sbe_ref, sbv_ref, sbx_ref, a_ref, wd_ref, bd_ref, o_ref, wdb_ref):
    nv = sbv_ref[pl.program_id(0)]

    @pl.when(nv > 0)
    def _():
        wdb_ref[...] = wd_ref[0].astype(BF16)

    def compute(r0, n):
        rows = pl.ds(r0, n)
        o_ref[rows, :] = jnp.dot(a_ref[rows, :], wdb_ref[...], preferred_element_type=F32) + bd_ref[0]

    _over_live_rows(nv, compute, o_ref)


def _expert_specs(d_in, tn, n_col_blocks, n_row_blocks):
    def col(j, sbv, i):
        return jnp.where(sbv[i] > 0, j, n_col_blocks - 1)

    x_spec = pl.BlockSpec((EXPERT_ROWS, d_in), lambda i, j, sbe, sbv, sbx: (sbx[i], 0),
                          pipeline_mode=pl.Buffered(1))
    w_spec = pl.BlockSpec((1, d_in, tn), lambda i, j, sbe, sbv, sbx: (sbe[i], 0, col(j, sbv, i)))
    b_spec = pl.BlockSpec((1, 1, tn), lambda i, j, sbe, sbv, sbx: (sbe[i], 0, col(j, sbv, i)))
    o_spec = pl.BlockSpec((EXPERT_ROWS, tn), lambda i, j, sbe, sbv, sbx: (
        jnp.where(sbv[i] > 0, i, n_row_blocks), jnp.where(sbv[i] > 0, j, 0)))
    return x_spec, w_spec, b_spec, o_spec


def _expert_in(meta, xg, w_gate, w_up, b_gate, b_up, tn=256):
    n_rows, d = xg.shape
    d_e = w_gate.shape[2]
    n_row_blocks = n_rows // EXPERT_ROWS - 1
    x_spec, w_spec, b_spec, o_spec = _expert_specs(d, tn, d_e // tn, n_row_blocks)
    return pl.pallas_call(
        _expert_in_kernel,
        grid_spec=pltpu.PrefetchScalarGridSpec(
            num_scalar_prefetch=3,
            grid=(n_row_blocks, d_e // tn),
            in_specs=[x_spec, w_spec, w_spec, b_spec, b_spec],
            out_specs=o_spec,
            scratch_shapes=[pltpu.VMEM((d, tn), BF16), pltpu.VMEM((d, tn), BF16)]),
        out_shape=jax.ShapeDtypeStruct((n_rows, d_e), BF16),
        compiler_params=_cparams(("arbitrary", "arbitrary")),
        name="expert_in",
    )(*meta, xg, w_gate, w_up, b_gate, b_up)


def _expert_out(meta, act, w_down, b_down, tn=512):
    n_rows, d_e = act.shape
    d = w_down.shape[2]
    n_row_blocks = n_rows // EXPERT_ROWS - 1
    x_spec, w_spec, b_spec, o_spec = _expert_specs(d_e, tn, d // tn, n_row_blocks)
    return pl.pallas_call(
        _expert_out_kernel,
        grid_spec=pltpu.PrefetchScalarGridSpec(
            num_scalar_prefetch=3,
            grid=(n_row_blocks, d // tn),
            in_specs=[x_spec, w_spec, b_spec],
            out_specs=o_spec,
            scratch_shapes=[pltpu.VMEM((d_e, tn), BF16)]),
        out_shape=jax.ShapeDtypeStruct((n_rows, d), F32),
        compiler_params=_cparams(("arbitrary", "arbitrary")),
        name="expert_out",
    )(*meta, act, w_down, b_down)


def _combine_kernel(pos_ref, y_hbm, gates_ref, x1_ref, gpost_ref, gate2_ref, oa_ref, ob_ref, buf_ref, sem,
                    *, blocks_a):
    i = pl.program_id(0)
    n = COMBINE_TOKENS * TOP_K

    def fetch(blk, slot):
        def issue(r, carry):
            _row_copy(y_hbm, buf_ref.at[slot], sem.at[slot], pos_ref[blk * n + r], r).start()
            return carry

        lax.fori_loop(0, n, issue, 0, unroll=DMA_UNROLL)

    @pl.when(i == 0)
    def _():
        fetch(0, 0)

    slot = lax.rem(i, 2)

    @pl.when(i + 1 < pl.num_programs(0))
    def _():
        fetch(i + 1, 1 - slot)

    pltpu.make_async_copy(y_hbm.at[pl.ds(0, n)], buf_ref.at[slot], sem.at[slot]).wait()
    gates = gates_ref[...]
    moe = gates[:, 0:1] * buf_ref[slot, 0:COMBINE_TOKENS]
    for k in range(1, TOP_K):
        moe = moe + gates[:, k:k + 1] * buf_ref[slot, k * COMBINE_TOKENS:(k + 1) * COMBINE_TOKENS]
    out = x1_ref[...] + gate2_ref[0] * (_rms_rows(moe) * gpost_ref[...])

    @pl.when(i < blocks_a)
    def _():
        oa_ref[...] = out

    @pl.when(i >= blocks_a)
    def _():
        ob_ref[...] = out


def _combine(pos, yb, gates, x1, g_post, gate2, group_of_block, rows_a):
    m, d = x1.shape
    tm = COMBINE_TOKENS
    blocks_a = rows_a // tm
    pos_blocked = pos.reshape(m // tm, tm, TOP_K).transpose(0, 2, 1).reshape(-1)
    return pl.pallas_call(
        functools.partial(_combine_kernel, blocks_a=blocks_a),
        grid_spec=pltpu.PrefetchScalarGridSpec(
            num_scalar_prefetch=1,
            grid=(m // tm,),
            in_specs=[pl.BlockSpec(memory_space=pl.ANY),
                      pl.BlockSpec((tm, LANES), lambda i, pos: (i, 0)),
                      pl.BlockSpec((tm, d), lambda i, pos: (i, 0)),
                      pl.BlockSpec((1, d), lambda i, pos: (0, 0)),
                      pl.BlockSpec((1, 1, d), lambda i, pos: (group_of_block(i, tm), 0, 0))],
            out_specs=[pl.BlockSpec((tm, d), lambda i, pos: (jnp.minimum(i, blocks_a - 1), 0)),
                       pl.BlockSpec((tm, d), lambda i, pos: (jnp.maximum(i - blocks_a, 0), 0))],
            scratch_shapes=[pltpu.VMEM((2, TOP_K * tm, d), F32), pltpu.SemaphoreType.DMA((2,))]),
        out_shape=[jax.ShapeDtypeStruct((rows_a, d), F32), jax.ShapeDtypeStruct((m - rows_a, d), F32)],
        compiler_params=_cparams(("arbitrary",)),
        name="moe_combine",
    )(pos_blocked, yb, gates, x1, g_post.reshape(1, d), gate2)


def _routing_tables(top_i, rank, counts, n_tok):
    m = n_tok * TOP_K
    n_sb = m // EXPERT_ROWS + N_EXPERTS
    n_rows = (n_sb + 1) * EXPERT_ROWS
    flat_e = top_i.reshape(-1)
    rank = rank.reshape(-1)
    pcounts = (counts + EXPERT_ROWS - 1) // EXPERT_ROWS * EXPERT_ROWS
    pend = jnp.cumsum(pcounts)
    pstart = pend - pcounts
    start = jnp.cumsum(counts) - counts
    pos = (pstart[flat_e] + rank).astype(jnp.int32)
    row_tok = jnp.zeros((m,), jnp.int32).at[start[flat_e] + rank].set(jnp.arange(m, dtype=jnp.int32) // TOP_K)
    sb_row = jnp.arange(n_sb, dtype=jnp.int32) * EXPERT_ROWS
    sb_e = jnp.minimum(jnp.searchsorted(pend, sb_row, side='right'), N_EXPERTS - 1).astype(jnp.int32)
    sb_off = sb_row - pstart[sb_e]
    sb_valid = jnp.clip(counts[sb_e] - sb_off, 0, EXPERT_ROWS)
    sb_valid = jnp.where(sb_row < pend[-1], sb_valid, 0).astype(jnp.int32)
    n_live = (pend[-1] // EXPERT_ROWS).astype(jnp.int32)
    sb_x = jnp.minimum(jnp.arange(n_sb, dtype=jnp.int32), jnp.maximum(n_live - 1, 0))
    chunks_per_block = EXPERT_ROWS // ROW_CHUNK
    chunk_off = jnp.arange(chunks_per_block, dtype=jnp.int32) * ROW_CHUNK
    chunk_valid = jnp.clip(sb_valid[:, None] - chunk_off[None, :], 0, ROW_CHUNK).reshape(-1).astype(jnp.int32)
    chunk_src = ((start[sb_e] + sb_off)[:, None] + chunk_off[None, :]).reshape(-1).astype(jnp.int32)
    chunk_src = jnp.where(chunk_valid > 0, chunk_src, 0)
    n_live_chunks = (n_live * chunks_per_block).reshape(1)
    return pos, row_tok, chunk_valid, chunk_src, n_live_chunks, (sb_e[sb_x], sb_valid, sb_x), n_rows


def _rot_cols(w):
    a1, a2, b1, b2 = jnp.split(w, 4, axis=-1)
    return jnp.concatenate([-a2, a1, -b2, b1], axis=-1)


def _rope_angles(n_tok):
    rows = n_tok // GRID_W
    row = jnp.repeat(jnp.arange(rows), GRID_W).astype(F32)
    col = jnp.tile(jnp.arange(GRID_W), rows).astype(F32)
    half = QK_ROPE // 2
    inv = ROPE_THETA ** (-jnp.arange(0, half, 2, dtype=F32) / half)
    ang_r = row[:, None] * inv
    ang_c = col[:, None] * inv
    ang = jnp.concatenate([ang_r, ang_r, ang_c, ang_c], axis=-1)
    return jnp.cos(ang), jnp.sin(ang)


def kernel(x_prompt, x_sample, cache_ckv, cache_krope, state_ssm_fwd, state_ssm_bwd, c, c_ctx, w_mod, b_mod, norm_mix_pre, norm_mix_post, norm_ffn_pre, norm_ffn_post, w_in, norm_q, norm_kv, w_uq, w_uk, w_uv, conv_w, conv_b, dt_bias_fwd, dt_bias_bwd, a_log_fwd, a_log_bwd, d_skip, norm_ssm, w_out, router_w, router_b, w_gate, b_gate, w_up, b_up, w_down, b_down):
    assert w_mod.shape[0] == 1, "single-layer trunk"
    n_ctx_seq, ctx_len, d = x_prompt.shape
    n_lat_seq, lat_len, _ = x_sample.shape
    past_len = cache_ckv.shape[2]
    n_ctx = n_ctx_seq * ctx_len
    n_lat = n_lat_seq * lat_len
    n_tok = n_ctx + n_lat
    n_groups = 1 + n_lat_seq

    def group_of_block(i, tm):
        row = i * tm
        return jnp.maximum(row - n_ctx, 0) // lat_len + (row >= n_ctx).astype(jnp.int32)

    cond = jnp.concatenate([c_ctx[None, :], c, jnp.zeros((SUBLANES - n_groups, d), F32)], axis=0)
    mod = _modulation(cond, w_mod[0], b_mod[0])[:n_groups]
    shift1, scale1, gate1, shift2, scale2, gate2 = [t.reshape(n_groups, 1, d) for t in jnp.split(mod, 6, axis=-1)]

    x_ctx = x_prompt.reshape(n_ctx, d)
    x_lat = x_sample.reshape(n_lat, d)
    h = _prenorm(x_ctx, x_lat, norm_mix_pre[0], scale1, shift1, group_of_block)

    o1, o2, o3, o4, o5 = Q_LORA, Q_LORA + KV_LORA, Q_LORA + KV_LORA + QK_ROPE, \
        Q_LORA + KV_LORA + QK_ROPE + SSM_WIDTH, Q_LORA + KV_LORA + QK_ROPE + SSM_WIDTH + CONV_DIM
    w_in0 = w_in[0]
    w_kr = w_in0[:, o2:o3]
    w_krdt = jnp.concatenate([w_kr, _rot_cols(w_kr), w_in0[:, o5:],
                              jnp.zeros((d, 2 * LANES - 2 * QK_ROPE - 2 * SSM_HEADS), F32)], axis=1)
    cq = _matmul(h, _column_blocks(w_in0[:, :o1], Q_LORA), F32, 1024, "in_proj_q")
    ckv = _matmul(h, _column_blocks(w_in0[:, o1:o2], KV_LORA), F32, 1024, "in_proj_kv")
    z = _matmul(h, _column_blocks(w_in0[:, o3:o4], 4 * LANES), F32, 1024, "in_proj_z")
    xbc = _matmul(h, _column_blocks(w_in0[:, o4:o5], 4 * LANES), F32, 1024, "in_proj_xbc")
    krdt = _matmul(h, _column_blocks(w_krdt, 2 * LANES), F32, 1024, "in_proj_krdt")

    cos, sin = _rope_angles(lat_len)
    ones = jnp.ones((n_ctx, QK_ROPE), F32)
    q_table = (QK_SCALE * math.log2(math.e)) * jnp.concatenate([
        jnp.concatenate([jnp.ones((n_ctx, QK_NOPE), F32), ones, 0.0 * ones], axis=1),
        jnp.tile(jnp.concatenate([jnp.ones((lat_len, QK_NOPE), F32), cos, sin], axis=1), (n_lat_seq, 1))], axis=0)
    k_table = jnp.concatenate([jnp.concatenate([ones, 0.0 * ones], axis=1),
                               jnp.tile(jnp.concatenate([cos, sin], axis=1), (n_lat_seq, 1))], axis=0)
    wq = w_uq[0].reshape(Q_LORA, MLA_HEADS, QK_NOPE + QK_ROPE)
    wq = jnp.concatenate([wq, _rot_cols(wq[..., QK_NOPE:])], axis=-1).reshape(Q_LORA, MLA_HEADS * QK_PAD)
    q = _q_proj(cq, norm_q[0], wq.astype(BF16), q_table)
    wk = w_uk[0].astype(BF16)
    wv = w_uv[0].astype(BF16)
    kk = _krope(krdt, k_table)
    ckr = cache_krope[:, 0].reshape(n_lat_seq * past_len, QK_ROPE)
    kk_c = jnp.concatenate([ckr, ckr], axis=1).astype(BF16)
    ckv_n, k, v = _kv_proj(ckv, norm_kv[0], wk, wv, kk, True, "kv_proj")
    _, k_c, v_c = _kv_proj(cache_ckv[:, 0].reshape(n_lat_seq * past_len, KV_LORA), norm_kv[0], wk, wv, kk_c,
                           False, "kv_proj_cache")

    attn_ctx = _attention(q, k, v, n_ctx_seq, ctx_len, ctx_len, 0, ctx_len, MLA_HEADS, "attn_ctx")

    def with_cache(lat, cached):
        lat = lat[n_ctx:].reshape(n_lat_seq, lat_len, -1)
        cached = cached.reshape(n_lat_seq, past_len, -1)
        return jnp.concatenate([lat, cached], axis=1).reshape(n_lat_seq * (lat_len + past_len), -1)

    attn_lat = _attention(q, with_cache(k, k_c), with_cache(v, v_c),
                          n_lat_seq, lat_len, lat_len + past_len, n_ctx, 2 * LANES, 1, "attn_lat")

    dt_bias = jnp.concatenate([dt_bias_fwd[0], dt_bias_bwd[0], jnp.zeros((LANES - 2 * SSM_HEADS,), F32)])
    a_log = jnp.concatenate([a_log_fwd[0], a_log_bwd[0], jnp.zeros((LANES - 2 * SSM_HEADS,), F32)])
    dt_arrays = _dt_prep(krdt, dt_bias.reshape(1, LANES), a_log.reshape(1, LANES))
    d_wide = jnp.repeat(d_skip[0], SSM_HEADDIM).reshape(1, SSM_WIDTH)
    zero_state = jnp.zeros((n_ctx_seq, SSM_WIDTH, D_STATE), F32)
    ssm = []
    fins = []
    for n_seq, seq_len, row0, tc, init_f, init_b in (
            (n_ctx_seq, ctx_len, 0, 4 * LANES, zero_state, zero_state),
            (n_lat_seq, lat_len, n_ctx, 2 * LANES,
             state_ssm_fwd[:, 0].reshape(n_lat_seq, SSM_WIDTH, D_STATE),
             state_ssm_bwd[:, 0].reshape(n_lat_seq, SSM_WIDTH, D_STATE))):
        tag = "ctx" if row0 == 0 else "lat"
        act = _conv_silu(xbc, conv_w[0], conv_b[0], n_seq, seq_len, row0, tc, "conv_" + tag)
        y_f, y_b, fin_f, fin_b = _ssd_scan(act, dt_arrays, init_f, init_b, n_seq, seq_len, row0, "ssd_" + tag)
        ssm.append(_ssd_out(y_f, y_b, act, z, d_wide, norm_ssm[0], row0))
        fins.append((fin_f, fin_b))

    attn = jnp.concatenate([attn_ctx, attn_lat], axis=0)
    ssm = jnp.concatenate(ssm, axis=0)
    w_o = w_out[0]
    mix = _matmul2(attn, ssm, _column_blocks(w_o[:MLA_WIDTH], 4 * LANES), _column_blocks(w_o[MLA_WIDTH:], 4 * LANES),
                   F32, 1024, "out_proj")
    rw = jnp.concatenate([router_w[0], jnp.zeros((d, LANES - N_EXPERTS), F32)], axis=1)
    rb = jnp.concatenate([router_b[0], jnp.zeros((LANES - N_EXPERTS,), F32)]).reshape(1, LANES)
    x1, h2, top_i, gates, rank, counts = _postmix_router(
        x_ctx, x_lat, mix, norm_mix_post[0], gate1, norm_ffn_pre[0], scale2, shift2, rw, rb, group_of_block)

    pos, row_tok, chunk_valid, chunk_src, n_live_chunks, meta, n_rows = _routing_tables(
        top_i[:, :TOP_K], rank[:, :TOP_K], counts[0, :N_EXPERTS], n_tok)
    xg = _gather_rows(row_tok, chunk_valid, chunk_src, n_live_chunks, h2, n_rows)
    act = _expert_in(meta, xg, w_gate[0], w_up[0], b_gate[0].reshape(N_EXPERTS, 1, -1),
                     b_up[0].reshape(N_EXPERTS, 1, -1))
    yb = _expert_out(meta, act, w_down[0], b_down[0].reshape(N_EXPERTS, 1, -1))
    y_ctx, y_lat = _combine(pos, yb, gates, x1, norm_ffn_post[0], gate2, group_of_block, n_ctx)

    y_prompt = y_ctx.reshape(n_ctx_seq, ctx_len, d)
    y_sample = y_lat.reshape(n_lat_seq, lat_len, d)
    new_ckv = ckv_n[:n_ctx].reshape(n_ctx_seq, 1, ctx_len, KV_LORA)
    new_kr = krdt[:n_ctx, :QK_ROPE].reshape(n_ctx_seq, 1, ctx_len, QK_ROPE)
    fin_f, fin_b = fins[0]
    new_f = fin_f.reshape(n_ctx_seq, 1, SSM_HEADS, SSM_HEADDIM, D_STATE)
    new_b = fin_b.reshape(n_ctx_seq, 1, SSM_HEADS, SSM_HEADDIM, D_STATE)
    return (y_prompt, y_sample, new_ckv, new_kr, new_f, new_b)
```

```python
import functools
import math

import jax
import jax.numpy as jnp
from jax import lax
from jax.experimental import pallas as pl
from jax.experimental.pallas import tpu as pltpu

F32 = jnp.float32
BF16 = jnp.bfloat16
HIGHEST = lax.Precision.HIGHEST

D_MODEL = 4096
GRID_W = 64
NORM_EPS = 1e-6
ROPE_THETA = 10000.0
MLA_HEADS = 16
QK_NOPE = 128
QK_ROPE = 64
V_HEAD = 128
Q_LORA = 768
KV_LORA = 512
MLA_WIDTH = MLA_HEADS * V_HEAD
QK_SCALE = (QK_NOPE + QK_ROPE) ** -0.5
SSM_HEADDIM = 64
SSM_HEADS = 32
SSM_WIDTH = SSM_HEADS * SSM_HEADDIM
SSM_GROUPS = 4
HEADS_PER_GROUP = SSM_HEADS // SSM_GROUPS
GROUP_WIDTH = SSM_WIDTH // SSM_GROUPS
D_STATE = 128
D_CONV = 5
CHUNK = 128
CONV_DIM = SSM_WIDTH + 2 * SSM_GROUPS * D_STATE
N_EXPERTS = 32
TOP_K = 4
SWIGLU_LIMIT = 7.0
SWIGLU_ALPHA = 1.702

LANES = 128
SUBLANES = 8
BF16_ROWS = 16
V7X_VMEM_LIMIT_BYTES = 56 * 1024 * 1024

QK_PAD = 2 * LANES
EXPERT_ROWS = 2304
ROW_CHUNK = 256
COMBINE_TOKENS = 64
DMA_UNROLL = 8


def _cparams(sem, vmem=V7X_VMEM_LIMIT_BYTES):
    return pltpu.CompilerParams(dimension_semantics=sem, vmem_limit_bytes=vmem)


def _sigmoid(x):
    return 1.0 / (1.0 + jnp.exp(-x))


def _rms_rows(x):
    return x * lax.rsqrt(jnp.mean(x * x, axis=-1, keepdims=True) + NORM_EPS)


def _mod_kernel(c_ref, w_ref, b_ref, o_ref):
    c = c_ref[...]
    s = (c * _sigmoid(c)).astype(BF16)
    o_ref[...] = jnp.dot(s, w_ref[...].astype(BF16), preferred_element_type=F32) + b_ref[...]


def _modulation(cond, w_mod, b_mod, tn=512):
    rows, k = cond.shape
    n = w_mod.shape[1]
    return pl.pallas_call(
        _mod_kernel,
        grid=(n // tn,),
        in_specs=[pl.BlockSpec((rows, k), lambda j: (0, 0)),
                  pl.BlockSpec((k, tn), lambda j: (0, j)),
                  pl.BlockSpec((1, tn), lambda j: (0, j))],
        out_specs=pl.BlockSpec((rows, tn), lambda j: (0, j)),
        out_shape=jax.ShapeDtypeStruct((rows, n), F32),
        compiler_params=_cparams(("parallel",)),
        name="modulation",
    )(cond, w_mod, b_mod.reshape(1, n))


def _prenorm_kernel(xa_ref, xb_ref, g_ref, sc_ref, sh_ref, o_ref, *, blocks_a):
    x = jnp.where(pl.program_id(0) < blocks_a, xa_ref[...], xb_ref[...])
    h = _rms_rows(x) * g_ref[...]
    o_ref[...] = (h * (1.0 + sc_ref[0]) + sh_ref[0]).astype(o_ref.dtype)


def _prenorm(xa, xb, gain, scale, shift, group_of_block, tm=256):
    d = xa.shape[1]
    m = xa.shape[0] + xb.shape[0]
    blocks_a = xa.shape[0] // tm
    return pl.pallas_call(
        functools.partial(_prenorm_kernel, blocks_a=blocks_a),
        grid=(m // tm,),
        in_specs=[pl.BlockSpec((tm, d), lambda i: (jnp.minimum(i, blocks_a - 1), 0)),
                  pl.BlockSpec((tm, d), lambda i: (jnp.maximum(i - blocks_a, 0), 0)),
                  pl.BlockSpec((1, d), lambda i: (0, 0)),
                  pl.BlockSpec((1, 1, d), lambda i: (group_of_block(i, tm), 0, 0)),
                  pl.BlockSpec((1, 1, d), lambda i: (group_of_block(i, tm), 0, 0))],
        out_specs=pl.BlockSpec((tm, d), lambda i: (i, 0)),
        out_shape=jax.ShapeDtypeStruct((m, d), BF16),
        compiler_params=_cparams(("parallel",)),
        name="prenorm",
    )(xa, xb, gain.reshape(1, d), scale, shift)


def _column_blocks(w, tn):
    k, n = w.shape
    return w.astype(BF16).reshape(k, n // tn, tn).transpose(1, 0, 2)


def _mm_kernel(a_ref, b_ref, o_ref):
    for r in range(a_ref.shape[0] // ROW_CHUNK):
        rows = slice(r * ROW_CHUNK, (r + 1) * ROW_CHUNK)
        o_ref[rows, :] = jnp.dot(a_ref[rows, :], b_ref[0], preferred_element_type=F32).astype(o_ref.dtype)


def _matmul(a, b_blocks, out_dtype, tm, name):
    m, k = a.shape
    n_blocks, _, tn = b_blocks.shape
    return pl.pallas_call(
        _mm_kernel,
        grid=(m // tm, n_blocks),
        in_specs=[pl.BlockSpec((tm, k), lambda i, j: (i, 0)),
                  pl.BlockSpec((1, k, tn), lambda i, j: (j, 0, 0))],
        out_specs=pl.BlockSpec((tm, tn), lambda i, j: (i, j)),
        out_shape=jax.ShapeDtypeStruct((m, n_blocks * tn), out_dtype),
        compiler_params=_cparams(("parallel", "parallel")),
        name=name,
    )(a, b_blocks)


def _mm2_kernel(a1a_ref, a1b_ref, a2a_ref, a2b_ref, b1_ref, b2_ref, o_ref, *, blocks_a):
    def project(a1_ref, a2_ref):
        for r in range(a1_ref.shape[0] // ROW_CHUNK):
            rows = slice(r * ROW_CHUNK, (r + 1) * ROW_CHUNK)
            acc = jnp.dot(a1_ref[rows, :], b1_ref[0], preferred_element_type=F32)
            acc = acc + jnp.dot(a2_ref[rows, :], b2_ref[0], preferred_element_type=F32)
            o_ref[rows, :] = acc.astype(o_ref.dtype)

    @pl.when(pl.program_id(0) < blocks_a)
    def _():
        project(a1a_ref, a2a_ref)

    @pl.when(pl.program_id(0) >= blocks_a)
    def _():
        project(a1b_ref, a2b_ref)


def _matmul2(a1_parts, a2_parts, b1_blocks, b2_blocks, out_dtype, tm, name):
    a1a, a1b = a1_parts
    a2a, a2b = a2_parts
    k1 = a1a.shape[1]
    k2 = a2a.shape[1]
    m = a1a.shape[0] + a1b.shape[0]
    blocks_a = a1a.shape[0] // tm
    n_blocks, _, tn = b1_blocks.shape
    first = lambda i, j: (jnp.minimum(i, blocks_a - 1), 0)
    second = lambda i, j: (jnp.maximum(i - blocks_a, 0), 0)
    return pl.pallas_call(
        functools.partial(_mm2_kernel, blocks_a=blocks_a),
        grid=(m // tm, n_blocks),
        in_specs=[pl.BlockSpec((tm, k1), first), pl.BlockSpec((tm, k1), second),
                  pl.BlockSpec((tm, k2), first), pl.BlockSpec((tm, k2), second),
                  pl.BlockSpec((1, k1, tn), lambda i, j: (j, 0, 0)),
                  pl.BlockSpec((1, k2, tn), lambda i, j: (j, 0, 0))],
        out_specs=pl.BlockSpec((tm, tn), lambda i, j: (i, j)),
        out_shape=jax.ShapeDtypeStruct((m, n_blocks * tn), out_dtype),
        compiler_params=_cparams(("parallel", "parallel")),
        name=name,
    )(a1a, a1b, a2a, a2b, b1_blocks, b2_blocks)


def _q_kernel(cq_ref, g_ref, w_ref, tab_ref, o_ref):
    xn = (_rms_rows(cq_ref[...]) * g_ref[...]).astype(BF16)
    tab = tab_ref[...]
    for h in range(MLA_HEADS):
        sl = slice(h * QK_PAD, (h + 1) * QK_PAD)
        q = jnp.dot(xn, w_ref[:, sl], preferred_element_type=F32)
        o_ref[:, sl] = (q * tab).astype(o_ref.dtype)


def _q_proj(cq, norm_q, w_q, q_table, tm=512):
    m, k = cq.shape
    n = w_q.shape[1]
    return pl.pallas_call(
        _q_kernel,
        grid=(m // tm,),
        in_specs=[pl.BlockSpec((tm, k), lambda i: (i, 0)),
                  pl.BlockSpec((1, k), lambda i: (0, 0)),
                  pl.BlockSpec((k, n), lambda i: (0, 0)),
                  pl.BlockSpec((tm, QK_PAD), lambda i: (i, 0))],
        out_specs=pl.BlockSpec((tm, n), lambda i: (i, 0)),
        out_shape=jax.ShapeDtypeStruct((m, n), BF16),
        compiler_params=_cparams(("parallel",)),
        name="q_proj",
    )(cq, norm_q.reshape(1, k), w_q, q_table)


def _kv_kernel(ckv_ref, g_ref, wk_ref, wv_ref, kk_ref, n_ref, k_ref, v_ref, *, normalise):
    x = ckv_ref[...]
    if normalise:
        x = _rms_rows(x) * g_ref[...]
    n_ref[...] = x
    xb = x.astype(BF16)
    kk = kk_ref[...]
    for h in range(MLA_HEADS):
        ws = slice(h * QK_NOPE, (h + 1) * QK_NOPE)
        k_ref[:, h * QK_PAD:h * QK_PAD + QK_NOPE] = jnp.dot(
            xb, wk_ref[:, ws], preferred_element_type=F32).astype(k_ref.dtype)
        k_ref[:, h * QK_PAD + QK_NOPE:(h + 1) * QK_PAD] = kk
    v_ref[...] = jnp.dot(xb, wv_ref[...], preferred_element_type=F32).astype(v_ref.dtype)


def _kv_proj(ckv, norm_kv, w_k, w_v, kk, normalise, name, tm=512):
    m, k = ckv.shape
    n = w_k.shape[1]
    return pl.pallas_call(
        functools.partial(_kv_kernel, normalise=normalise),
        grid=(m // tm,),
        in_specs=[pl.BlockSpec((tm, k), lambda i: (i, 0)),
                  pl.BlockSpec((1, k), lambda i: (0, 0)),
                  pl.BlockSpec((k, n), lambda i: (0, 0)),
                  pl.BlockSpec((k, n), lambda i: (0, 0)),
                  pl.BlockSpec((tm, LANES), lambda i: (i, 0))],
        out_specs=[pl.BlockSpec((tm, k), lambda i: (i, 0)),
                   pl.BlockSpec((tm, MLA_HEADS * QK_PAD), lambda i: (i, 0)),
                   pl.BlockSpec((tm, n), lambda i: (i, 0))],
        out_shape=[jax.ShapeDtypeStruct((m, k), F32),
                   jax.ShapeDtypeStruct((m, MLA_HEADS * QK_PAD), BF16),
                   jax.ShapeDtypeStruct((m, n), BF16)],
        compiler_params=_cparams(("parallel",)),
        name=name,
    )(ckv, norm_kv.reshape(1, k), w_k, w_v, kk)


def _krope_kernel(kr_ref, tab_ref, o_ref):
    t = kr_ref[...] * tab_ref[...]
    o_ref[...] = (t + pltpu.roll(t, QK_ROPE, 1)).astype(o_ref.dtype)


def _krope(krdt, k_table, tm=1024):
    m = krdt.shape[0]
    return pl.pallas_call(
        _krope_kernel,
        grid=(m // tm,),
        in_specs=[pl.BlockSpec((tm, LANES), lambda i: (i, 0)),
                  pl.BlockSpec((tm, LANES), lambda i: (i, 0))],
        out_specs=pl.BlockSpec((tm, LANES), lambda i: (i, 0)),
        out_shape=jax.ShapeDtypeStruct((m, LANES), BF16),
        compiler_params=_cparams(("parallel",)),
        name="krope",
    )(krdt, k_table)


def _attn_kernel(q_ref, k_ref, v_ref, o_ref, *, heads):
    for h in range(heads):
        q = q_ref[:, h * QK_PAD:(h + 1) * QK_PAD]
        k = k_ref[:, h * QK_PAD:(h + 1) * QK_PAD]
        s = lax.dot_general(q, k, (((1,), (1,)), ((), ())), preferred_element_type=F32)
        p = jnp.exp2(s - jnp.max(s, axis=1, keepdims=True))
        denom = jnp.sum(p, axis=1, keepdims=True)
        o = jnp.dot(p.astype(BF16), v_ref[:, h * V_HEAD:(h + 1) * V_HEAD], preferred_element_type=F32)
        o_ref[:, h * V_HEAD:(h + 1) * V_HEAD] = (o / denom).astype(o_ref.dtype)


def _attn_cached_kernel(q_ref, k_ref, kc_ref, v_ref, vc_ref, o_ref):
    contract_last = (((1,), (1,)), ((), ()))
    q = q_ref[...]
    s_own = lax.dot_general(q, k_ref[...], contract_last, preferred_element_type=F32)
    s_old = lax.dot_general(q, kc_ref[...], contract_last, preferred_element_type=F32)
    top = jnp.maximum(jnp.max(s_own, axis=1, keepdims=True), jnp.max(s_old, axis=1, keepdims=True))
    p_own = jnp.exp2(s_own - top)
    p_old = jnp.exp2(s_old - top)
    denom = jnp.sum(p_own, axis=1, keepdims=True) + jnp.sum(p_old, axis=1, keepdims=True)
    o = jnp.dot(p_own.astype(BF16), v_ref[...], preferred_element_type=F32)
    o = o + jnp.dot(p_old.astype(BF16), vc_ref[...], preferred_element_type=F32)
    o_ref[...] = (o / denom).astype(o_ref.dtype)


def _attention_cached(q, k, v, k_cache, v_cache, n_batch, s_len, past_len, row0, tq, name):
    nq = s_len // tq
    q_blk0 = row0 // tq
    kv_blk0 = row0 // s_len
    return pl.pallas_call(
        _attn_cached_kernel,
        grid=(n_batch, MLA_HEADS, nq),
        in_specs=[pl.BlockSpec((tq, QK_PAD), lambda b, h, qi: (q_blk0 + b * nq + qi, h)),
                  pl.BlockSpec((s_len, QK_PAD), lambda b, h, qi: (kv_blk0 + b, h)),
                  pl.BlockSpec((past_len, QK_PAD), lambda b, h, qi: (b, h)),
                  pl.BlockSpec((s_len, V_HEAD), lambda b, h, qi: (kv_blk0 + b, h)),
                  pl.BlockSpec((past_len, V_HEAD), lambda b, h, qi: (b, h))],
        out_specs=pl.BlockSpec((tq, V_HEAD), lambda b, h, qi: (b * nq + qi, h)),
        out_shape=jax.ShapeDtypeStruct((n_batch * s_len, MLA_WIDTH), BF16),
        compiler_params=_cparams(("parallel", "parallel", "parallel")),
        name=name,
    )(q, k, k_cache, v, v_cache)


def _attention(q, k, v, n_batch, s_len, kv_len, q_row0, tq, heads, name):
    nq = s_len // tq
    q_blk0 = q_row0 // tq
    return pl.pallas_call(
        functools.partial(_attn_kernel, heads=heads),
        grid=(n_batch, MLA_HEADS // heads, nq),
        in_specs=[pl.BlockSpec((tq, heads * QK_PAD), lambda b, h, qi: (q_blk0 + b * nq + qi, h)),
                  pl.BlockSpec((kv_len, heads * QK_PAD), lambda b, h, qi: (b, h)),
                  pl.BlockSpec((kv_len, heads * V_HEAD), lambda b, h, qi: (b, h))],
        out_specs=pl.BlockSpec((tq, heads * V_HEAD), lambda b, h, qi: (b * nq + qi, h)),
        out_shape=jax.ShapeDtypeStruct((n_batch * s_len, MLA_WIDTH), BF16),
        compiler_params=_cparams(("parallel", "parallel", "parallel")),
        name=name,
    )(q, k, v)


def _conv_kernel(u_ref, w_ref, b_ref, o_ref, pad_ref, *, seq_len):
    halo = SUBLANES
    width = u_ref.shape[-1]
    pad_ref[0:halo, :] = jnp.zeros((halo, width), F32)
    pad_ref[halo + seq_len:2 * halo + seq_len, :] = jnp.zeros((halo, width), F32)
    pad_ref[halo:halo + seq_len, :] = u_ref[...]
    for r in range(seq_len // ROW_CHUNK):
        acc = b_ref[...]
        for k in range(D_CONV):
            start = halo - D_CONV // 2 + k + r * ROW_CHUNK
            acc = acc + w_ref[k:k + 1, :] * pad_ref[start:start + ROW_CHUNK, :]
        o_ref[r * ROW_CHUNK:(r + 1) * ROW_CHUNK, :] = acc * _sigmoid(acc)


def _conv_silu(xbc, conv_w, conv_b, n_seq, seq_len, row0, tc, name):
    width = xbc.shape[1]
    blk0 = row0 // seq_len
    return pl.pallas_call(
        functools.partial(_conv_kernel, seq_len=seq_len),
        grid=(n_seq, width // tc),
        in_specs=[pl.BlockSpec((seq_len, tc), lambda s, c: (blk0 + s, c)),
                  pl.BlockSpec((D_CONV, tc), lambda s, c: (0, c)),
                  pl.BlockSpec((1, tc), lambda s, c: (0, c))],
        out_specs=pl.BlockSpec((seq_len, tc), lambda s, c: (s, c)),
        out_shape=jax.ShapeDtypeStruct((n_seq * seq_len, width), F32),
        scratch_shapes=[pltpu.VMEM((seq_len + 2 * SUBLANES, tc), F32)],
        compiler_params=_cparams(("parallel", "parallel")),
        name=name,
    )(xbc, conv_w, conv_b.reshape(1, width))


def _dt_kernel(raw_ref, bias_ref, alog_ref, dtf_ref, csf_ref, cstf_ref, dtb_ref, csb_ref, cstb_ref):
    raw = raw_ref[...] + bias_ref[...]
    dt = jnp.maximum(raw, 0.0) + jnp.log1p(jnp.exp(-jnp.abs(raw)))
    a = dt * (-jnp.exp(alog_ref[...]))
    row = lax.broadcasted_iota(jnp.int32, (CHUNK, CHUNK), 0)
    col = lax.broadcasted_iota(jnp.int32, (CHUNK, CHUNK), 1)
    lower = (row >= col).astype(F32)
    upper = (row <= col).astype(F32)
    prefix = jnp.dot(lower, a, precision=HIGHEST, preferred_element_type=F32)
    suffix = jnp.dot(upper, a, precision=HIGHEST, preferred_element_type=F32)
    dtf_ref[...] = dt
    csf_ref[...] = prefix
    cstf_ref[...] = prefix.T
    suffix_b = pltpu.roll(suffix, LANES - SSM_HEADS, 1)
    dtb_ref[...] = pltpu.roll(dt, LANES - SSM_HEADS, 1)
    csb_ref[...] = suffix_b
    cstb_ref[...] = suffix_b.T


def _dt_prep(krdt, dt_bias, a_log):
    m = krdt.shape[0]
    shp = jax.ShapeDtypeStruct((m, LANES), F32)
    spec = pl.BlockSpec((CHUNK, LANES), lambda i: (i, 0))
    return pl.pallas_call(
        _dt_kernel,
        grid=(m // CHUNK,),
        in_specs=[pl.BlockSpec((CHUNK, LANES), lambda i: (i, 1)),
                  pl.BlockSpec((1, LANES), lambda i: (0, 0)),
                  pl.BlockSpec((1, LANES), lambda i: (0, 0))],
        out_specs=[spec] * 6,
        out_shape=[shp] * 6,
        compiler_params=_cparams(("parallel",)),
        name="dt_prep",
    )(krdt, dt_bias, a_log)


def _replicate_lanes(x, e_ref):
    hi = x.astype(BF16)
    rest = x - hi.astype(F32)
    mid = rest.astype(BF16)
    lo = (rest - mid.astype(F32)).astype(BF16)
    e = e_ref[...]
    out = jnp.dot(hi, e, preferred_element_type=F32)
    out = out + jnp.dot(mid, e, preferred_element_type=F32)
    return out + jnp.dot(lo, e, preferred_element_type=F32)


def _ssd_direction(x_ref, b_ref, c_ref, dt_ref, cs_ref, cst_ref, y_ref, st_ref, e_head_ref, e_lane_ref,
                   reverse):
    ii = lax.broadcasted_iota(jnp.int32, (CHUNK, CHUNK), 0)
    jj = lax.broadcasted_iota(jnp.int32, (CHUNK, CHUNK), 1)
    mask = (jj >= ii) if reverse else (ii >= jj)
    cs = cs_ref[...]
    cst = cst_ref[...]
    edge = cs[0:BF16_ROWS, :] if reverse else cs[CHUNK - BF16_ROWS:CHUNK, :]
    wide = _replicate_lanes(jnp.concatenate([dt_ref[...], cs, edge], axis=0), e_head_ref)
    dt_w = wide[0:CHUNK]
    p_w = wide[CHUNK:2 * CHUNK]
    edge_row = 2 * CHUNK + (0 if reverse else BF16_ROWS - 1)
    tot_w = wide[edge_row:edge_row + 1]
    p_rep = _replicate_lanes(cs, e_lane_ref)
    xdt = x_ref[...] * dt_w
    xdec = (xdt * jnp.exp(tot_w - p_w)).astype(BF16)
    scale_in = jnp.exp(p_w)
    decay_tot = jnp.exp(tot_w)
    lane_w = lax.broadcasted_iota(jnp.int32, xdt.shape, 1)
    first_of_pair = (lane_w & SSM_HEADDIM) == 0
    x_first = jnp.where(first_of_pair, xdt, 0.0).astype(BF16)
    x_second = jnp.where(first_of_pair, 0.0, xdt).astype(BF16)
    for g in range(SSM_GROUPS):
        gs = slice(g * D_STATE, (g + 1) * D_STATE)
        gw = slice(g * GROUP_WIDTH, (g + 1) * GROUP_WIDTH)
        b_f32 = b_ref[:, gs]
        bg = b_f32.astype(BF16)
        bgt = b_f32.T.astype(BF16)
        cg = c_ref[:, gs].astype(BF16)
        cb = lax.dot_general(cg, bg, (((1,), (1,)), ((), ())), preferred_element_type=F32)
        st_old = st_ref[:, gw]
        y_off = jnp.dot(cg, st_old.astype(BF16), preferred_element_type=F32)
        st_ref[:, gw] = st_old * decay_tot[:, gw] + jnp.dot(bgt, xdec[:, gw], preferred_element_type=F32)
        for pair in range(HEADS_PER_GROUP // 2):
            h0 = g * HEADS_PER_GROUP + 2 * pair
            ps = slice(h0 * SSM_HEADDIM, (h0 + 2) * SSM_HEADDIM)
            acc = None
            for h, xm in ((h0, x_first), (h0 + 1, x_second)):
                w = jnp.where(mask, jnp.exp(p_rep[:, h * LANES:(h + 1) * LANES] - cst[h:h + 1, :]), 0.0)
                term = jnp.dot((cb * w).astype(BF16), xm[:, ps], preferred_element_type=F32)
                acc = term if acc is None else acc + term
            y_ref[:, ps] = acc + y_off[:, 2 * pair * SSM_HEADDIM:(2 * pair + 2) * SSM_HEADDIM] * scale_in[:, ps]


def _ssd_kernel(xf_ref, bf_ref, cf_ref, dtf_ref, csf_ref, cstf_ref, initf_ref,
                xb_ref, bb_ref, cb_ref, dtb_ref, csb_ref, cstb_ref, initb_ref,
                yf_ref, yb_ref, finf_ref, finb_ref, stf_ref, stb_ref, e_head_ref, e_lane_ref, *, n_chunks):
    c = pl.program_id(1)

    @pl.when(c == 0)
    def _():
        stf_ref[...] = initf_ref[0].T
        stb_ref[...] = initb_ref[0].T
        row = lax.broadcasted_iota(jnp.int32, e_head_ref.shape, 0)
        col = lax.broadcasted_iota(jnp.int32, e_head_ref.shape, 1)
        e_head_ref[...] = jnp.where((col >> (SSM_HEADDIM.bit_length() - 1)) == row, 1.0, 0.0).astype(BF16)
        row = lax.broadcasted_iota(jnp.int32, e_lane_ref.shape, 0)
        col = lax.broadcasted_iota(jnp.int32, e_lane_ref.shape, 1)
        e_lane_ref[...] = jnp.where((col >> (LANES.bit_length() - 1)) == row, 1.0, 0.0).astype(BF16)

    _ssd_direction(xf_ref, bf_ref, cf_ref, dtf_ref, csf_ref, cstf_ref, yf_ref, stf_ref,
                   e_head_ref, e_lane_ref, False)
    _ssd_direction(xb_ref, bb_ref, cb_ref, dtb_ref, csb_ref, cstb_ref, yb_ref, stb_ref,
                   e_head_ref, e_lane_ref, True)

    @pl.when(c == n_chunks - 1)
    def _():
        finf_ref[0] = stf_ref[...].T
        finb_ref[0] = stb_ref[...].T


def _ssd_scan(xbc, dt_arrays, init_f, init_b, n_seq, seq_len, row0, name):
    n_chunks = seq_len // CHUNK
    blk0 = row0 // CHUNK
    xblk = SSM_WIDTH // (SSM_GROUPS * D_STATE)
    dt_f, cs_f, cst_f, dt_b, cs_b, cst_b = dt_arrays

    def direction_specs(chunk):
        small = pl.BlockSpec((CHUNK, LANES), lambda s, c: (blk0 + chunk(s, c), 0))
        return [pl.BlockSpec((CHUNK, SSM_WIDTH), lambda s, c: (chunk(s, c), 0)),
                pl.BlockSpec((CHUNK, SSM_GROUPS * D_STATE), lambda s, c: (chunk(s, c), xblk)),
                pl.BlockSpec((CHUNK, SSM_GROUPS * D_STATE), lambda s, c: (chunk(s, c), xblk + 1)),
                small, small, small,
                pl.BlockSpec((1, SSM_WIDTH, D_STATE), lambda s, c: (s, 0, 0))]

    fwd = lambda s, c: s * n_chunks + c
    bwd = lambda s, c: s * n_chunks + (n_chunks - 1 - c)
    state_spec = pl.BlockSpec((1, SSM_WIDTH, D_STATE), lambda s, c: (s, 0, 0))
    y_shape = jax.ShapeDtypeStruct((n_seq * seq_len, SSM_WIDTH), F32)
    state_shape = jax.ShapeDtypeStruct((n_seq, SSM_WIDTH, D_STATE), F32)
    return pl.pallas_call(
        functools.partial(_ssd_kernel, n_chunks=n_chunks),
        grid=(n_seq, n_chunks),
        in_specs=direction_specs(fwd) + direction_specs(bwd),
        out_specs=[pl.BlockSpec((CHUNK, SSM_WIDTH), lambda s, c: (fwd(s, c), 0)),
                   pl.BlockSpec((CHUNK, SSM_WIDTH), lambda s, c: (bwd(s, c), 0)),
                   state_spec, state_spec],
        out_shape=[y_shape, y_shape, state_shape, state_shape],
        scratch_shapes=[pltpu.VMEM((D_STATE, SSM_WIDTH), F32), pltpu.VMEM((D_STATE, SSM_WIDTH), F32),
                        pltpu.VMEM((LANES, SSM_WIDTH), BF16), pltpu.VMEM((LANES, SSM_HEADS * LANES), BF16)],
        compiler_params=_cparams(("parallel", "arbitrary")),
        name=name,
    )(xbc, xbc, xbc, dt_f, cs_f, cst_f, init_f, xbc, xbc, xbc, dt_b, cs_b, cst_b, init_b)


def _ssd_out_kernel(yf_ref, yb_ref, x_ref, z_ref, d_ref, g_ref, o_ref):
    for g in range(SSM_GROUPS):
        gs = slice(g * GROUP_WIDTH, (g + 1) * GROUP_WIDTH)
        z = z_ref[:, gs]
        y = (yf_ref[:, gs] + yb_ref[:, gs] + d_ref[:, gs] * x_ref[:, gs]) * (z * _sigmoid(z))
        o_ref[:, gs] = (_rms_rows(y) * g_ref[:, gs]).astype(o_ref.dtype)


def _ssd_out(y_f, y_b, xbc, z, d_wide, norm_ssm, row0, tm=256):
    m = y_f.shape[0]
    blk0 = row0 // tm
    return pl.pallas_call(
        _ssd_out_kernel,
        grid=(m // tm,),
        in_specs=[pl.BlockSpec((tm, SSM_WIDTH), lambda i: (i, 0)),
                  pl.BlockSpec((tm, SSM_WIDTH), lambda i: (i, 0)),
                  pl.BlockSpec((tm, SSM_WIDTH), lambda i: (i, 0)),
                  pl.BlockSpec((tm, SSM_WIDTH), lambda i: (blk0 + i, 0)),
                  pl.BlockSpec((1, SSM_WIDTH), lambda i: (0, 0)),
                  pl.BlockSpec((1, SSM_WIDTH), lambda i: (0, 0))],
        out_specs=pl.BlockSpec((tm, SSM_WIDTH), lambda i: (i, 0)),
        out_shape=jax.ShapeDtypeStruct((m, SSM_WIDTH), BF16),
        compiler_params=_cparams(("parallel",)),
        name="ssd_out",
    )(y_f, y_b, xbc, z, d_wide, norm_ssm.reshape(1, SSM_WIDTH))


def _postmix_kernel(xa_ref, xb_ref, mix_ref, gpost_ref, gate1_ref, gpre_ref, sc_ref, sh_ref, rw_ref, rb_ref,
                    x1_ref, h_ref, idx_ref, gates_ref, rank_ref, counts_ref, run_ref, *, blocks_a):
    i = pl.program_id(0)

    @pl.when(i == 0)
    def _():
        run_ref[...] = jnp.zeros(run_ref.shape, F32)

    x = jnp.where(i < blocks_a, xa_ref[...], xb_ref[...])
    x1 = x + gate1_ref[0] * (_rms_rows(mix_ref[...]) * gpost_ref[...])
    x1_ref[...] = x1
    h = (_rms_rows(x1) * gpre_ref[...]) * (1.0 + sc_ref[0]) + sh_ref[0]
    h_ref[...] = h
    logits = jnp.dot(h, rw_ref[...], precision=HIGHEST, preferred_element_type=F32) + rb_ref[...]
    lane = lax.broadcasted_iota(jnp.int32, logits.shape, 1)
    lane_f = lane.astype(F32)
    live = jnp.where(lane < N_EXPERTS, logits, -jnp.inf)
    vals, idxs = [], []
    for _ in range(TOP_K):
        top = jnp.max(live, axis=1, keepdims=True)
        first = jnp.min(jnp.where(live == top, lane_f, float(LANES)), axis=1, keepdims=True)
        vals.append(top)
        idxs.append(first)
        live = jnp.where(lane_f == first, -jnp.inf, live)
    exps = [jnp.exp(v - vals[0]) for v in vals]
    denom = exps[0]
    for e in exps[1:]:
        denom = denom + e
    gate_out = jnp.zeros(logits.shape, F32)
    idx_out = jnp.zeros(logits.shape, F32)
    for k in range(TOP_K):
        gate_out = jnp.where(lane == k, exps[k] / denom, gate_out)
        idx_out = jnp.where(lane == k, idxs[k], idx_out)
    gates_ref[...] = gate_out
    idx_ref[...] = idx_out.astype(jnp.int32)

    hots = [jnp.where(lane_f == idxs[k], 1.0, 0.0) for k in range(TOP_K)]
    multi = hots[0]
    for hot in hots[1:]:
        multi = multi + hot
    n_rows = multi.shape[0]
    earlier = (lax.broadcasted_iota(jnp.int32, (n_rows, n_rows), 0)
               > lax.broadcasted_iota(jnp.int32, (n_rows, n_rows), 1))
    before = run_ref[0:1, :] + jnp.dot(jnp.where(earlier, 1.0, 0.0).astype(BF16), multi.astype(BF16),
                                       preferred_element_type=F32)
    rank_out = jnp.zeros(logits.shape, F32)
    for k in range(TOP_K):
        rank_out = jnp.where(lane == k, jnp.sum(hots[k] * before, axis=1, keepdims=True), rank_out)
    rank_ref[...] = rank_out.astype(jnp.int32)
    run_ref[...] = run_ref[...] + jnp.sum(multi, axis=0, keepdims=True)
    counts_ref[...] = run_ref[...].astype(jnp.int32)


def _postmix_router(xa, xb, mix, g_post, gate1, g_pre, scale2, shift2, router_w, router_b,
                    group_of_block, tm=128):
    d = xa.shape[1]
    m = xa.shape[0] + xb.shape[0]
    blocks_a = xa.shape[0] // tm
    row = lambda i: (i, 0)
    fixed = lambda i: (0, 0)
    grp = lambda i: (group_of_block(i, tm), 0, 0)
    wide = jax.ShapeDtypeStruct((m, d), F32)
    narrow_i = jax.ShapeDtypeStruct((m, LANES), jnp.int32)
    return pl.pallas_call(
        functools.partial(_postmix_kernel, blocks_a=blocks_a),
        grid=(m // tm,),
        in_specs=[pl.BlockSpec((tm, d), lambda i: (jnp.minimum(i, blocks_a - 1), 0)),
                  pl.BlockSpec((tm, d), lambda i: (jnp.maximum(i - blocks_a, 0), 0)),
                  pl.BlockSpec((tm, d), row),
                  pl.BlockSpec((1, d), fixed), pl.BlockSpec((1, 1, d), grp),
                  pl.BlockSpec((1, d), fixed), pl.BlockSpec((1, 1, d), grp), pl.BlockSpec((1, 1, d), grp),
                  pl.BlockSpec((d, LANES), fixed), pl.BlockSpec((1, LANES), fixed)],
        out_specs=[pl.BlockSpec((tm, d), row), pl.BlockSpec((tm, d), row),
                   pl.BlockSpec((tm, LANES), row), pl.BlockSpec((tm, LANES), row),
                   pl.BlockSpec((tm, LANES), row), pl.BlockSpec((SUBLANES, LANES), fixed)],
        out_shape=[wide, wide, narrow_i, jax.ShapeDtypeStruct((m, LANES), F32), narrow_i,
                   jax.ShapeDtypeStruct((SUBLANES, LANES), jnp.int32)],
        scratch_shapes=[pltpu.VMEM((SUBLANES, LANES), F32)],
        compiler_params=_cparams(("arbitrary",)),
        name="postmix_router",
    )(xa, xb, mix, g_post.reshape(1, d), gate1, g_pre.reshape(1, d), scale2, shift2, router_w, router_b)


def _row_copy(src_hbm, dst_vmem, sem, src_row, dst_row):
    return pltpu.make_async_copy(src_hbm.at[pl.ds(src_row, 1)], dst_vmem.at[pl.ds(dst_row, 1)], sem)


def _gather_kernel(tok_ref, nv_ref, src_ref, nlive_ref, h_hbm, o_ref, buf_ref, sem):
    i = pl.program_id(0)

    def fetch(blk, slot):
        n_live = nv_ref[blk]
        base = src_ref[blk]

        def issue(r, carry):
            _row_copy(h_hbm, buf_ref.at[slot], sem.at[slot], tok_ref[base + r], r).start()
            return carry

        @pl.when(n_live == ROW_CHUNK)
        def _():
            lax.fori_loop(0, ROW_CHUNK, issue, 0, unroll=DMA_UNROLL)

        @pl.when((n_live > 0) & (n_live < ROW_CHUNK))
        def _():
            buf_ref[slot] = jnp.zeros(buf_ref.shape[1:], F32)
            lax.fori_loop(0, n_live, issue, 0)

    @pl.when(i == 0)
    def _():
        fetch(0, 0)

    slot = lax.rem(i, 2)

    @pl.when(i + 1 < pl.num_programs(0))
    def _():
        fetch(i + 1, 1 - slot)

    nv = nv_ref[i]

    @pl.when(nv == ROW_CHUNK)
    def _():
        pltpu.make_async_copy(h_hbm.at[pl.ds(0, ROW_CHUNK)], buf_ref.at[slot], sem.at[slot]).wait()

    @pl.when((nv > 0) & (nv < ROW_CHUNK))
    def _():
        def drain(r, carry):
            _row_copy(h_hbm, buf_ref.at[slot], sem.at[slot], 0, r).wait()
            return carry

        lax.fori_loop(0, nv, drain, 0)

    @pl.when(nv > 0)
    def _():
        o_ref[...] = buf_ref[slot].astype(o_ref.dtype)

    @pl.when(nv == 0)
    def _():
        o_ref[...] = jnp.zeros(o_ref.shape, o_ref.dtype)


def _gather_rows(row_tok, chunk_valid, chunk_src, n_live_chunks, h, n_rows):
    d = h.shape[1]
    n_chunks = n_rows // ROW_CHUNK - EXPERT_ROWS // ROW_CHUNK
    return pl.pallas_call(
        _gather_kernel,
        grid_spec=pltpu.PrefetchScalarGridSpec(
            num_scalar_prefetch=4,
            grid=(n_chunks,),
            in_specs=[pl.BlockSpec(memory_space=pl.ANY)],
            out_specs=pl.BlockSpec((ROW_CHUNK, d),
                                   lambda i, tok, nv, src, nlive: (jnp.where(i < nlive[0], i, n_chunks), 0)),
            scratch_shapes=[pltpu.VMEM((2, ROW_CHUNK, d), F32), pltpu.SemaphoreType.DMA((2,))]),
        out_shape=jax.ShapeDtypeStruct((n_rows, d), BF16),
        compiler_params=_cparams(("arbitrary",)),
        name="moe_gather",
    )(row_tok, chunk_valid, chunk_src, n_live_chunks, h)


def _over_live_rows(nv, compute, o_ref):
    n_chunks = (nv + ROW_CHUNK - 1) // ROW_CHUNK
    n_quads = n_chunks // 4

    def quad(p, carry):
        compute(pl.multiple_of(p * (4 * ROW_CHUNK), 4 * ROW_CHUNK), 4 * ROW_CHUNK)
        return carry

    lax.fori_loop(0, n_quads, quad, 0)
    done = n_quads * 4

    @pl.when(n_chunks - done >= 2)
    def _():
        compute(pl.multiple_of(done * ROW_CHUNK, 2 * ROW_CHUNK), 2 * ROW_CHUNK)

    @pl.when(n_chunks % 2 == 1)
    def _():
        compute(pl.multiple_of((n_chunks - 1) * ROW_CHUNK, ROW_CHUNK), ROW_CHUNK)

    def zero(c, carry):
        o_ref[pl.ds(pl.multiple_of(c * ROW_CHUNK, ROW_CHUNK), ROW_CHUNK), :] = jnp.zeros(
            (ROW_CHUNK, o_ref.shape[1]), o_ref.dtype)
        return carry

    lax.fori_loop(n_chunks, EXPERT_ROWS // ROW_CHUNK, zero, 0)


def _expert_in_kernel(sbe_ref, sbv_ref, sbx_ref, x_ref, wg_ref, wu_ref, bg_ref, bu_ref, o_ref,
                      wgb_ref, wub_ref):
    nv = sbv_ref[pl.program_id(0)]

    @pl.when(nv > 0)
    def _():
        wgb_ref[...] = wg_ref[0].astype(BF16)
        wub_ref[...] = wu_ref[0].astype(BF16)

    def compute(r0, n):
        rows = pl.ds(r0, n)
        xs = x_ref[rows, :]
        g = jnp.dot(xs, wgb_ref[...], preferred_element_type=F32) + bg_ref[0]
        u = jnp.dot(xs, wub_ref[...], preferred_element_type=F32) + bu_ref[0]
        g = jnp.minimum(g, SWIGLU_LIMIT)
        u = jnp.clip(u, -SWIGLU_LIMIT, SWIGLU_LIMIT)
        o_ref[rows, :] = (g * _sigmoid(SWIGLU_ALPHA * g) * (u + 1.0)).astype(o_ref.dtype)

    _over_live_rows(nv, compute, o_ref)


def _expert_out_kernel(sbe_ref, sbv_ref, sbx_ref, a_ref, wd_ref, bd_ref, o_ref, wdb_ref):
    nv = sbv_ref[pl.program_id(0)]

    @pl.when(nv > 0)
    def _():
        wdb_ref[...] = wd_ref[0].astype(BF16)

    def compute(r0, n):
        rows = pl.ds(r0, n)
        o_ref[rows, :] = jnp.dot(a_ref[rows, :], wdb_ref[...], preferred_element_type=F32) + bd_ref[0]

    _over_live_rows(nv, compute, o_ref)


def _expert_specs(d_in, tn, n_col_blocks, n_row_blocks):
    def col(j, sbv, i):
        return jnp.where(sbv[i] > 0, j, n_col_blocks - 1)

    x_spec = pl.BlockSpec((EXPERT_ROWS, d_in), lambda i, j, sbe, sbv, sbx: (sbx[i], 0),
                          pipeline_mode=pl.Buffered(1))
    w_spec = pl.BlockSpec((1, d_in, tn), lambda i, j, sbe, sbv, sbx: (sbe[i], 0, col(j, sbv, i)))
    b_spec = pl.BlockSpec((1, 1, tn), lambda i, j, sbe, sbv, sbx: (sbe[i], 0, col(j, sbv, i)))
    o_spec = pl.BlockSpec((EXPERT_ROWS, tn), lambda i, j, sbe, sbv, sbx: (
        jnp.where(sbv[i] > 0, i, n_row_blocks), jnp.where(sbv[i] > 0, j, 0)))
    return x_spec, w_spec, b_spec, o_spec


def _expert_in(meta, xg, w_gate, w_up, b_gate, b_up, tn=256):
    n_rows, d = xg.shape
    d_e = w_gate.shape[2]
    n_row_blocks = n_rows // EXPERT_ROWS - 1
    x_spec, w_spec, b_spec, o_spec = _expert_specs(d, tn, d_e // tn, n_row_blocks)
    return pl.pallas_call(
        _expert_in_kernel,
        grid_spec=pltpu.PrefetchScalarGridSpec(
            num_scalar_prefetch=3,
            grid=(n_row_blocks, d_e // tn),
            in_specs=[x_spec, w_spec, w_spec, b_spec, b_spec],
            out_specs=o_spec,
            scratch_shapes=[pltpu.VMEM((d, tn), BF16), pltpu.VMEM((d, tn), BF16)]),
        out_shape=jax.ShapeDtypeStruct((n_rows, d_e), BF16),
        compiler_params=_cparams(("arbitrary", "arbitrary")),
        name="expert_in",
    )(*meta, xg, w_gate, w_up, b_gate, b_up)


def _expert_out(meta, act, w_down, b_down, tn=512):
    n_rows, d_e = act.shape
    d = w_down.shape[2]
    n_row_blocks = n_rows // EXPERT_ROWS - 1
    x_spec, w_spec, b_spec, o_spec = _expert_specs(d_e, tn, d // tn, n_row_blocks)
    return pl.pallas_call(
        _expert_out_kernel,
        grid_spec=pltpu.PrefetchScalarGridSpec(
            num_scalar_prefetch=3,
            grid=(n_row_blocks, d // tn),
            in_specs=[x_spec, w_spec, b_spec],
            out_specs=o_spec,
            scratch_shapes=[pltpu.VMEM((d_e, tn), BF16)]),
        out_shape=jax.ShapeDtypeStruct((n_rows, d), F32),
        compiler_params=_cparams(("arbitrary", "arbitrary")),
        name="expert_out",
    )(*meta, act, w_down, b_down)


def _combine_kernel(pos_ref, y_hbm, gates_ref, x1_ref, gpost_ref, gate2_ref, oa_ref, ob_ref, buf_ref, sem,
                    *, blocks_a):
    i = pl.program_id(0)
    n = COMBINE_TOKENS * TOP_K

    def fetch(blk, slot):
        def issue(r, carry):
            _row_copy(y_hbm, buf_ref.at[slot], sem.at[slot], pos_ref[blk * n + r], r).start()
            return carry

        lax.fori_loop(0, n, issue, 0, unroll=DMA_UNROLL)

    @pl.when(i == 0)
    def _():
        fetch(0, 0)

    slot = lax.rem(i, 2)

    @pl.when(i + 1 < pl.num_programs(0))
    def _():
        fetch(i + 1, 1 - slot)

    pltpu.make_async_copy(y_hbm.at[pl.ds(0, n)], buf_ref.at[slot], sem.at[slot]).wait()
    gates = gates_ref[...]
    moe = gates[:, 0:1] * buf_ref[slot, 0:COMBINE_TOKENS]
    for k in range(1, TOP_K):
        moe = moe + gates[:, k:k + 1] * buf_ref[slot, k * COMBINE_TOKENS:(k + 1) * COMBINE_TOKENS]
    out = x1_ref[...] + gate2_ref[0] * (_rms_rows(moe) * gpost_ref[...])

    @pl.when(i < blocks_a)
    def _():
        oa_ref[...] = out

    @pl.when(i >= blocks_a)
    def _():
        ob_ref[...] = out


def _combine(pos, yb, gates, x1, g_post, gate2, group_of_block, rows_a):
    m, d = x1.shape
    tm = COMBINE_TOKENS
    blocks_a = rows_a // tm
    pos_blocked = pos.reshape(m // tm, tm, TOP_K).transpose(0, 2, 1).reshape(-1)
    return pl.pallas_call(
        functools.partial(_combine_kernel, blocks_a=blocks_a),
        grid_spec=pltpu.PrefetchScalarGridSpec(
            num_scalar_prefetch=1,
            grid=(m // tm,),
            in_specs=[pl.BlockSpec(memory_space=pl.ANY),
                      pl.BlockSpec((tm, LANES), lambda i, pos: (i, 0)),
                      pl.BlockSpec((tm, d), lambda i, pos: (i, 0)),
                      pl.BlockSpec((1, d), lambda i, pos: (0, 0)),
                      pl.BlockSpec((1, 1, d), lambda i, pos: (group_of_block(i, tm), 0, 0))],
            out_specs=[pl.BlockSpec((tm, d), lambda i, pos: (jnp.minimum(i, blocks_a - 1), 0)),
                       pl.BlockSpec((tm, d), lambda i, pos: (jnp.maximum(i - blocks_a, 0), 0))],
            scratch_shapes=[pltpu.VMEM((2, TOP_K * tm, d), F32), pltpu.SemaphoreType.DMA((2,))]),
        out_shape=[jax.ShapeDtypeStruct((rows_a, d), F32), jax.ShapeDtypeStruct((m - rows_a, d), F32)],
        compiler_params=_cparams(("arbitrary",)),
        name="moe_combine",
    )(pos_blocked, yb, gates, x1, g_post.reshape(1, d), gate2)


def _routing_tables(top_i, rank, counts, n_tok):
    m = n_tok * TOP_K
    n_sb = m // EXPERT_ROWS + N_EXPERTS
    n_rows = (n_sb + 1) * EXPERT_ROWS
    flat_e = top_i.reshape(-1)
    rank = rank.reshape(-1)
    pcounts = (counts + EXPERT_ROWS - 1) // EXPERT_ROWS * EXPERT_ROWS
    pend = jnp.cumsum(pcounts)
    pstart = pend - pcounts
    start = jnp.cumsum(counts) - counts
    pos = (pstart[flat_e] + rank).astype(jnp.int32)
    row_tok = jnp.zeros((m,), jnp.int32).at[start[flat_e] + rank].set(jnp.arange(m, dtype=jnp.int32) // TOP_K)
    sb_row = jnp.arange(n_sb, dtype=jnp.int32) * EXPERT_ROWS
    sb_e = jnp.minimum(jnp.searchsorted(pend, sb_row, side='right'), N_EXPERTS - 1).astype(jnp.int32)
    sb_off = sb_row - pstart[sb_e]
    sb_valid = jnp.clip(counts[sb_e] - sb_off, 0, EXPERT_ROWS)
    sb_valid = jnp.where(sb_row < pend[-1], sb_valid, 0).astype(jnp.int32)
    n_live = (pend[-1] // EXPERT_ROWS).astype(jnp.int32)
    sb_x = jnp.minimum(jnp.arange(n_sb, dtype=jnp.int32), jnp.maximum(n_live - 1, 0))
    chunks_per_block = EXPERT_ROWS // ROW_CHUNK
    chunk_off = jnp.arange(chunks_per_block, dtype=jnp.int32) * ROW_CHUNK
    chunk_valid = jnp.clip(sb_valid[:, None] - chunk_off[None, :], 0, ROW_CHUNK).reshape(-1).astype(jnp.int32)
    chunk_src = ((start[sb_e] + sb_off)[:, None] + chunk_off[None, :]).reshape(-1).astype(jnp.int32)
    chunk_src = jnp.where(chunk_valid > 0, chunk_src, 0)
    n_live_chunks = (n_live * chunks_per_block).reshape(1)
    return pos, row_tok, chunk_valid, chunk_src, n_live_chunks, (sb_e[sb_x], sb_valid, sb_x), n_rows


def _rot_cols(w):
    a1, a2, b1, b2 = jnp.split(w, 4, axis=-1)
    return jnp.concatenate([-a2, a1, -b2, b1], axis=-1)


def _rope_angles(n_tok):
    rows = n_tok // GRID_W
    row = jnp.repeat(jnp.arange(rows), GRID_W).astype(F32)
    col = jnp.tile(jnp.arange(GRID_W), rows).astype(F32)
    half = QK_ROPE // 2
    inv = ROPE_THETA ** (-jnp.arange(0, half, 2, dtype=F32) / half)
    ang_r = row[:, None] * inv
    ang_c = col[:, None] * inv
    ang = jnp.concatenate([ang_r, ang_r, ang_c, ang_c], axis=-1)
    return jnp.cos(ang), jnp.sin(ang)


def kernel(x_prompt, x_sample, cache_ckv, cache_krope, state_ssm_fwd, state_ssm_bwd, c, c_ctx, w_mod, b_mod, norm_mix_pre, norm_mix_post, norm_ffn_pre, norm_ffn_post, w_in, norm_q, norm_kv, w_uq, w_uk, w_uv, conv_w, conv_b, dt_bias_fwd, dt_bias_bwd, a_log_fwd, a_log_bwd, d_skip, norm_ssm, w_out, router_w, router_b, w_gate, b_gate, w_up, b_up, w_down, b_down):
    assert w_mod.shape[0] == 1, "single-layer trunk"
    n_ctx_seq, ctx_len, d = x_prompt.shape
    n_lat_seq, lat_len, _ = x_sample.shape
    past_len = cache_ckv.shape[2]
    n_ctx = n_ctx_seq * ctx_len
    n_lat = n_lat_seq * lat_len
    n_tok = n_ctx + n_lat
    n_groups = 1 + n_lat_seq
    assert n_ctx % lat_len == 0, "latent sequences must start on a latent-length row block"

    def group_of_block(i, tm):
        row = i * tm
        return jnp.maximum(row - n_ctx, 0) // lat_len + (row >= n_ctx).astype(jnp.int32)

    cond = jnp.concatenate([c_ctx[None, :], c, jnp.zeros((SUBLANES - n_groups, d), F32)], axis=0)
    mod = _modulation(cond, w_mod[0], b_mod[0])[:n_groups]
    shift1, scale1, gate1, shift2, scale2, gate2 = [t.reshape(n_groups, 1, d) for t in jnp.split(mod, 6, axis=-1)]

    x_ctx = x_prompt.reshape(n_ctx, d)
    x_lat = x_sample.reshape(n_lat, d)
    h = _prenorm(x_ctx, x_lat, norm_mix_pre[0], scale1, shift1, group_of_block)

    o1, o2, o3, o4, o5 = Q_LORA, Q_LORA + KV_LORA, Q_LORA + KV_LORA + QK_ROPE, \
        Q_LORA + KV_LORA + QK_ROPE + SSM_WIDTH, Q_LORA + KV_LORA + QK_ROPE + SSM_WIDTH + CONV_DIM
    w_in0 = w_in[0]
    w_kr = w_in0[:, o2:o3]
    w_krdt = jnp.concatenate([w_kr, _rot_cols(w_kr), w_in0[:, o5:],
                              jnp.zeros((d, 2 * LANES - 2 * QK_ROPE - 2 * SSM_HEADS), F32)], axis=1)
    cq = _matmul(h, _column_blocks(w_in0[:, :o1], Q_LORA), F32, 1024, "in_proj_q")
    ckv = _matmul(h, _column_blocks(w_in0[:, o1:o2], KV_LORA), F32, 1024, "in_proj_kv")
    z = _matmul(h, _column_blocks(w_in0[:, o3:o4], 4 * LANES), F32, 1024, "in_proj_z")
    xbc = _matmul(h, _column_blocks(w_in0[:, o4:o5], 4 * LANES), F32, 1024, "in_proj_xbc")
    krdt = _matmul(h, _column_blocks(w_krdt, 2 * LANES), F32, 1024, "in_proj_krdt")

    cos, sin = _rope_angles(lat_len)
    ones = jnp.ones((n_ctx, QK_ROPE), F32)
    q_table = (QK_SCALE * math.log2(math.e)) * jnp.concatenate([
        jnp.concatenate([jnp.ones((n_ctx, QK_NOPE), F32), ones, 0.0 * ones], axis=1),
        jnp.tile(jnp.concatenate([jnp.ones((lat_len, QK_NOPE), F32), cos, sin], axis=1), (n_lat_seq, 1))], axis=0)
    k_table = jnp.concatenate([jnp.concatenate([ones, 0.0 * ones], axis=1),
                               jnp.tile(jnp.concatenate([cos, sin], axis=1), (n_lat_seq, 1))], axis=0)
    wq = w_uq[0].reshape(Q_LORA, MLA_HEADS, QK_NOPE + QK_ROPE)
    wq = jnp.concatenate([wq, _rot_cols(wq[..., QK_NOPE:])], axis=-1).reshape(Q_LORA, MLA_HEADS * QK_PAD)
    q = _q_proj(cq, norm_q[0], wq.astype(BF16), q_table)
    wk = w_uk[0].astype(BF16)
    wv = w_uv[0].astype(BF16)
    kk = _krope(krdt, k_table)
    ckr = cache_krope[:, 0].reshape(n_lat_seq * past_len, QK_ROPE)
    kk_c = jnp.concatenate([ckr, ckr], axis=1).astype(BF16)
    ckv_n, k, v = _kv_proj(ckv, norm_kv[0], wk, wv, kk, True, "kv_proj")
    _, k_c, v_c = _kv_proj(cache_ckv[:, 0].reshape(n_lat_seq * past_len, KV_LORA), norm_kv[0], wk, wv, kk_c,
                           False, "kv_proj_cache")

    attn_ctx = _attention(q, k, v, n_ctx_seq, ctx_len, ctx_len, 0, ctx_len, MLA_HEADS, "attn_ctx")

    attn_lat = _attention_cached(q, k, v, k_c, v_c, n_lat_seq, lat_len, past_len, n_ctx, 2 * LANES, "attn_lat")

    dt_bias = jnp.concatenate([dt_bias_fwd[0], dt_bias_bwd[0], jnp.zeros((LANES - 2 * SSM_HEADS,), F32)])
    a_log = jnp.concatenate([a_log_fwd[0], a_log_bwd[0], jnp.zeros((LANES - 2 * SSM_HEADS,), F32)])
    dt_arrays = _dt_prep(krdt, dt_bias.reshape(1, LANES), a_log.reshape(1, LANES))
    d_wide = jnp.repeat(d_skip[0], SSM_HEADDIM).reshape(1, SSM_WIDTH)
    zero_state = jnp.zeros((n_ctx_seq, SSM_WIDTH, D_STATE), F32)
    ssm = []
    fins = []
    for n_seq, seq_len, row0, tc, init_f, init_b in (
            (n_ctx_seq, ctx_len, 0, 4 * LANES, zero_state, zero_state),
            (n_lat_seq, lat_len, n_ctx, 2 * LANES,
             state_ssm_fwd[:, 0].reshape(n_lat_seq, SSM_WIDTH, D_STATE),
             state_ssm_bwd[:, 0].reshape(n_lat_seq, SSM_WIDTH, D_STATE))):
        tag = "ctx" if row0 == 0 else "lat"
        act = _conv_silu(xbc, conv_w[0], conv_b[0], n_seq, seq_len, row0, tc, "conv_" + tag)
        y_f, y_b, fin_f, fin_b = _ssd_scan(act, dt_arrays, init_f, init_b, n_seq, seq_len, row0, "ssd_" + tag)
        ssm.append(_ssd_out(y_f, y_b, act, z, d_wide, norm_ssm[0], row0))
        fins.append((fin_f, fin_b))

    w_o = w_out[0]
    mix = _matmul2((attn_ctx, attn_lat), ssm, _column_blocks(w_o[:MLA_WIDTH], 4 * LANES),
                   _column_blocks(w_o[MLA_WIDTH:], 4 * LANES), F32, 1024, "out_proj")
    rw = jnp.concatenate([router_w[0], jnp.zeros((d, LANES - N_EXPERTS), F32)], axis=1)
    rb = jnp.concatenate([router_b[0], jnp.zeros((LANES - N_EXPERTS,), F32)]).reshape(1, LANES)
    x1, h2, top_i, gates, rank, counts = _postmix_router(
        x_ctx, x_lat, mix, norm_mix_post[0], gate1, norm_ffn_pre[0], scale2, shift2, rw, rb, group_of_block)

    pos, row_tok, chunk_valid, chunk_src, n_live_chunks, meta, n_rows = _routing_tables(
        top_i[:, :TOP_K], rank[:, :TOP_K], counts[0, :N_EXPERTS], n_tok)
    xg = _gather_rows(row_tok, chunk_valid, chunk_src, n_live_chunks, h2, n_rows)
    act = _expert_in(meta, xg, w_gate[0], w_up[0], b_gate[0].reshape(N_EXPERTS, 1, -1),
                     b_up[0].reshape(N_EXPERTS, 1, -1))
    yb = _expert_out(meta, act, w_down[0], b_down[0].reshape(N_EXPERTS, 1, -1))
    y_ctx, y_lat = _combine(pos, yb, gates, x1, norm_ffn_post[0], gate2, group_of_block, n_ctx)

    y_prompt = y_ctx.reshape(n_ctx_seq, ctx_len, d)
    y_sample = y_lat.reshape(n_lat_seq, lat_len, d)
    new_ckv = ckv_n[:n_ctx].reshape(n_ctx_seq, 1, ctx_len, KV_LORA)
    new_kr = krdt[:n_ctx, :QK_ROPE].reshape(n_ctx_seq, 1, ctx_len, QK_ROPE)
    fin_f, fin_b = fins[0]
    new_f = fin_f.reshape(n_ctx_seq, 1, SSM_HEADS, SSM_HEADDIM, D_STATE)
    new_b = fin_b.reshape(n_ctx_seq, 1, SSM_HEADS, SSM_HEADDIM, D_STATE)
    return (y_prompt, y_sample, new_ckv, new_kr, new_f, new_b)
```
